```python
import math
import jax, jax.numpy as jnp
from jax import lax
import numpy as np

D_MODEL = 1024
BATCH = 4
SEQ = 4096
DEPTH = 4
DEC_BATCH = 128
DEC_SEQ = 8
PAST_LEN = 2048
PAGE_SIZE = 128

N_MIXERS = 2
N_GLA = (DEPTH + 1) // 2
N_MOBA = DEPTH // 2
GLA_HEADS = 4
GLA_DK = D_MODEL // 2 // GLA_HEADS
GLA_DV = D_MODEL // GLA_HEADS
GLA_DK_TOT = GLA_HEADS * GLA_DK
GLA_DV_TOT = GLA_HEADS * GLA_DV
GLA_GATE_RANK = 16
GLA_GATE_TAU = 16.0
GLA_CHUNK = 64
MOBA_HEADS = 8
MOBA_HD = D_MODEL // MOBA_HEADS
MOBA_BLOCK = 256
MOBA_TOPK = 3
MOBA_QBLOCK = 16
MEM_LEN = 256
X_HEADS = 4
X_HD = 128
X_W = X_HEADS * X_HD
D_FF = 4 * D_MODEL
EPS = 1e-6
NEG_INF = -1e30

kernel_name = "gla_moba_hybrid_decode_step"


def _rmsnorm(x, g):
    xf = x.astype(jnp.float32)
    y = xf * lax.rsqrt(jnp.mean(xf * xf, axis=-1, keepdims=True) + EPS)
    return (y * g.astype(jnp.float32)).astype(x.dtype)


def _alibi_slopes(n):
    return jnp.asarray(2.0 ** (-8.0 * np.arange(1, n + 1) / n), dtype=jnp.float32)


def _gla_recurrence(q, k, v, logf, s0):
    B, L, H, DK = q.shape
    DV = v.shape[-1]
    C = math.gcd(L, GLA_CHUNK)
    n = L // C

    def chunks(a):
        return a.astype(jnp.float32).reshape(B, n, C, H, a.shape[-1]).swapaxes(0, 1)

    causal = jnp.tril(jnp.ones((C, C), dtype=bool))

    def step(S, inp):
        qc, kc, vc, gc = inp
        G = jnp.cumsum(gc, axis=1)
        G_last = G[:, -1]
        q_dec = qc * jnp.exp(G)
        k_dec = kc * jnp.exp(-G)
        o_inter = jnp.einsum('bchk,bhkv->bchv', q_dec, S)
        A = jnp.where(causal, jnp.einsum('bchk,bshk->bhcs', q_dec, k_dec), 0.0)
        o = o_inter + jnp.einsum('bhcs,bshv->bchv', A, vc)
        S_new = jnp.exp(G_last)[..., None] * S + jnp.einsum(
            'bchk,bchv->bhkv', kc * jnp.exp(G_last[:, None] - G), vc)
        return S_new, o

    S_fin, o = lax.scan(step, s0.astype(jnp.float32), (chunks(q), chunks(k), chunks(v), chunks(logf)))
    return o.swapaxes(0, 1).reshape(B, L, H, DV), S_fin


def _gla_mixer(h, s0, w_in, w_g1, w_g2, b_g, g_out, w_o):
    B, L, _ = h.shape
    q, k, v, r = jnp.split(h @ w_in, [GLA_DK_TOT, 2 * GLA_DK_TOT, 2 * GLA_DK_TOT + GLA_DV_TOT], axis=-1)
    q = q.reshape(B, L, GLA_HEADS, GLA_DK) * (GLA_DK ** -0.5)
    k = k.reshape(B, L, GLA_HEADS, GLA_DK)
    v = v.reshape(B, L, GLA_HEADS, GLA_DV)
    z = (h @ w_g1) @ w_g2 + b_g
    logf = (jax.nn.log_sigmoid(z.astype(jnp.float32)) / GLA_GATE_TAU).reshape(B, L, GLA_HEADS, GLA_DK)
    o, s_new = _gla_recurrence(q, k, v, logf, s0)
    o = _rmsnorm(o, g_out.reshape(GLA_HEADS, GLA_DV)).reshape(B, L, GLA_DV_TOT).astype(h.dtype)
    o = o * jax.nn.silu(r)
    return o @ w_o, s_new


def _moba_qkv(h, w_qkv):
    B, L, _ = h.shape
    q, k, v = jnp.split(h @ w_qkv, 3, axis=-1)
    shp = (B, L, MOBA_HEADS, MOBA_HD)
    return q.reshape(shp), k.reshape(shp), v.reshape(shp)


def _moba_attend(q, pos, k_own, v_own, own_pos, slopes, k_sel=None, v_sel=None, sel_pos=None, sel_valid=None):
    scale = MOBA_HD ** -0.5
    d_own = (pos[:, None] - own_pos[None, :]).astype(jnp.float32)
    l_own = jnp.einsum('bqhd,bshd->bqhs', q, k_own).astype(jnp.float32) * scale
    l_own = l_own - slopes[None, None, :, None] * d_own[None, :, None, :]
    l_own = jnp.where((d_own >= 0)[None, :, None, :], l_own, NEG_INF)
    if k_sel is None:
        p = jax.nn.softmax(l_own, axis=-1)
        return jnp.einsum('bqhs,bshd->bqhd', p.astype(v_own.dtype), v_own)
    B, Q, H, K, R, _ = k_sel.shape
    d_sel = (pos[None, :, None, None, None] - sel_pos).astype(jnp.float32)
    l_sel = jnp.einsum('bqhd,bqhkrd->bqhkr', q, k_sel).astype(jnp.float32) * scale
    l_sel = l_sel - slopes[None, None, :, None, None] * d_sel
    if sel_valid is not None:
        l_sel = jnp.where(sel_valid[..., None], l_sel, NEG_INF)
    p = jax.nn.softmax(jnp.concatenate([l_sel.reshape(B, Q, H, K * R), l_own], axis=-1), axis=-1)
    p_sel = p[..., :K * R].reshape(B, Q, H, K, R).astype(v_sel.dtype)
    p_own = p[..., K * R:].astype(v_own.dtype)
    return (jnp.einsum('bqhkr,bqhkrd->bqhd', p_sel, v_sel)
            + jnp.einsum('bqhs,bshd->bqhd', p_own, v_own))


def _moba_prompt(q, k, v, slopes):
    B, S, H, HD = q.shape
    nb = -(-S // MOBA_BLOCK)
    pad = nb * MOBA_BLOCK - S
    kp = jnp.pad(k, ((0, 0), (0, pad), (0, 0), (0, 0)))
    vp = jnp.pad(v, ((0, 0), (0, pad), (0, 0), (0, 0)))
    kb = kp.reshape(B, nb, MOBA_BLOCK, H, HD)
    vb = vp.reshape(B, nb, MOBA_BLOCK, H, HD)
    k_mean = jnp.mean(kb, axis=2, dtype=jnp.float32)
    topk = min(MOBA_TOPK, nb - 1)
    n_qb = S // MOBA_QBLOCK
    q_blocks = q.reshape(B, n_qb, MOBA_QBLOCK, H, HD).swapaxes(0, 1)
    b_ix = jnp.arange(B)[:, None, None, None]
    h_ix = jnp.arange(H)[None, None, :, None]
    blk_ids = jnp.arange(nb)

    def one_block(args):
        qi, qb = args
        start = qi * MOBA_QBLOCK
        pos = start + jnp.arange(MOBA_QBLOCK)
        c = start // MOBA_BLOCK
        own_start = c * MOBA_BLOCK
        k_own = lax.dynamic_slice_in_dim(kp, own_start, MOBA_BLOCK, axis=1)
        v_own = lax.dynamic_slice_in_dim(vp, own_start, MOBA_BLOCK, axis=1)
        own_pos = own_start + jnp.arange(MOBA_BLOCK)
        if topk == 0:
            return _moba_attend(qb, pos, k_own, v_own, own_pos, slopes)
        gate = jnp.einsum('bqhd,bnhd->bqhn', qb.astype(jnp.float32), k_mean)
        gate = jnp.where(blk_ids < c, gate, NEG_INF)
        _, idx = lax.top_k(gate, topk)
        valid = idx < c
        k_sel = kb[b_ix, idx, :, h_ix]
        v_sel = vb[b_ix, idx, :, h_ix]
        sel_pos = idx[..., None] * MOBA_BLOCK + jnp.arange(MOBA_BLOCK)
        return _moba_attend(qb, pos, k_own, v_own, own_pos, slopes, k_sel, v_sel, sel_pos, valid)

    out = lax.map(one_block, (jnp.arange(n_qb, dtype=jnp.int32), q_blocks))
    return out.swapaxes(0, 1).reshape(B, S, H, HD)


def _moba_sample(q, k_new, v_new, cache_k, cache_v, j, page_table, slopes):
    B, T, H, HD = q.shape
    ppb = MOBA_BLOCK // PAGE_SIZE
    n_full = PAST_LEN // MOBA_BLOCK
    own_start = n_full * MOBA_BLOCK
    pos = PAST_LEN + jnp.arange(T)
    own_pages = page_table[:, n_full * ppb:]
    n_own_rows = own_pages.shape[1] * PAGE_SIZE
    k_cur = cache_k[j, own_pages].reshape(B, n_own_rows, H, HD).astype(k_new.dtype)
    v_cur = cache_v[j, own_pages].reshape(B, n_own_rows, H, HD).astype(v_new.dtype)
    k_own = jnp.concatenate([k_cur, k_new], axis=1)
    v_own = jnp.concatenate([v_cur, v_new], axis=1)
    own_pos = own_start + jnp.arange(n_own_rows + T)
    topk = min(MOBA_TOPK, n_full)
    if topk == 0:
        return _moba_attend(q, pos, k_own, v_own, own_pos, slopes)
    past_pages = page_table[:, :n_full * ppb]
    page_mean = lax.map(lambda pt: jnp.mean(cache_k[j, pt], axis=1, dtype=jnp.float32), past_pages)
    k_mean = page_mean.reshape(B, n_full, ppb, H, HD).mean(axis=2)
    gate = jnp.einsum('bthd,bnhd->bthn', q.astype(jnp.float32), k_mean)
    _, idx = lax.top_k(gate, topk)
    b_ix = jnp.arange(B)[:, None, None, None, None]
    h_ix = jnp.arange(H)[None, None, :, None, None]

    def one_query(args):
        t, q_t, idx_t = args
        phys = page_table[b_ix, idx_t[..., None] * ppb + jnp.arange(ppb)]
        k_sel = cache_k[j, phys, :, h_ix].reshape(B, 1, H, topk, MOBA_BLOCK, HD).astype(q.dtype)
        v_sel = cache_v[j, phys, :, h_ix].reshape(B, 1, H, topk, MOBA_BLOCK, HD).astype(q.dtype)
        sel_pos = idx_t[..., None] * MOBA_BLOCK + jnp.arange(MOBA_BLOCK)
        pos_t = (PAST_LEN + t)[None]
        return _moba_attend(q_t, pos_t, k_own, v_own, own_pos, slopes, k_sel, v_sel, sel_pos)

    out = lax.map(one_query, (jnp.arange(T, dtype=jnp.int32), q.swapaxes(0, 1)[:, :, None],
                              idx.swapaxes(0, 1)[:, :, None]))
    return out[:, :, 0].swapaxes(0, 1)


def _mem_kv(mem, g, w_kv):
    B, M, _ = mem.shape
    k, v = jnp.split(_rmsnorm(mem, g) @ w_kv, 2, axis=-1)
    return k.reshape(B, M, X_HEADS, X_HD), v.reshape(B, M, X_HEADS, X_HD)


def _cross_attn(h, mk, mv, w_q, w_o):
    B, L, _ = h.shape
    q = (h @ w_q).reshape(B, L, X_HEADS, X_HD)
    logits = jnp.einsum('blhd,bmhd->bhlm', q, mk).astype(jnp.float32) * (X_HD ** -0.5)
    p = jax.nn.softmax(logits, axis=-1).astype(mv.dtype)
    return jnp.einsum('bhlm,bmhd->blhd', p, mv).reshape(B, L, X_W) @ w_o


def _sqrelu_mlp(h, w_up, w_down):
    a = jax.nn.relu(h @ w_up)
    return (a * a) @ w_down


def setup_inputs(seed: int = 0) -> dict:
    key = jax.random.key(seed)
    ks = iter(jax.random.split(key, 40))
    f32 = jnp.float32

    def nrm(shape, scale=1.0):
        return jax.random.normal(next(ks), shape, f32) * scale

    def gain(shape):
        return 1.0 + nrm(shape, 0.02)

    n_pages = PAST_LEN // PAGE_SIZE
    n_used = DEC_BATCH * n_pages
    n_pool = n_used + n_used // 4
    perm = jax.random.permutation(next(ks), n_pool)
    page_table = perm[:n_used].reshape(DEC_BATCH, n_pages).astype(jnp.int32)
    D = D_MODEL
    return {
        "x_prompt": nrm((BATCH, SEQ, D)),
        "x_sample": nrm((DEC_BATCH, DEC_SEQ, D)),
        "cache_moba_k": nrm((N_MOBA, n_pool, PAGE_SIZE, MOBA_HEADS, MOBA_HD)),
        "cache_moba_v": nrm((N_MOBA, n_pool, PAGE_SIZE, MOBA_HEADS, MOBA_HD)),
        "state_gla": nrm((N_GLA, DEC_BATCH, GLA_HEADS, GLA_DK, GLA_DV)),
        "cache_mem_k": nrm((DEPTH, DEC_BATCH, MEM_LEN, X_HEADS, X_HD)),
        "cache_mem_v": nrm((DEPTH, DEC_BATCH, MEM_LEN, X_HEADS, X_HD)),
        "page_table": page_table,
        "mem_prompt": nrm((BATCH, MEM_LEN, D)),
        "g_mix": gain((DEPTH, D)),
        "g_cross": gain((DEPTH, D)),
        "g_mlp": gain((DEPTH, D)),
        "g_final": gain((D,)),
        "w_gla_in": nrm((N_GLA, D, 2 * GLA_DK_TOT + 2 * GLA_DV_TOT), D ** -0.5),
        "w_gla_g1": nrm((N_GLA, D, GLA_GATE_RANK), D ** -0.5),
        "w_gla_g2": nrm((N_GLA, GLA_GATE_RANK, GLA_DK_TOT), GLA_GATE_RANK ** -0.5),
        "b_gla_g": nrm((N_GLA, GLA_DK_TOT), 0.01),
        "g_gla_out": gain((N_GLA, GLA_DV_TOT)),
        "w_gla_o": nrm((N_GLA, GLA_DV_TOT, D), GLA_DV_TOT ** -0.5),
        "w_moba_qkv": nrm((N_MOBA, D, 3 * MOBA_HEADS * MOBA_HD), D ** -0.5),
        "w_moba_o": nrm((N_MOBA, MOBA_HEADS * MOBA_HD, D), (MOBA_HEADS * MOBA_HD) ** -0.5),
        "g_mem": gain((DEPTH, D)),
        "w_mem_kv": nrm((DEPTH, D, 2 * X_W), D ** -0.5),
        "w_xq": nrm((DEPTH, D, X_W), D ** -0.5),
        "w_xo": nrm((DEPTH, X_W, D), X_W ** -0.5),
        "w_up": nrm((DEPTH, D, D_FF), D ** -0.5),
        "w_down": nrm((DEPTH, D_FF, D), D_FF ** -0.5),
    }


def reference(x_prompt, x_sample, cache_moba_k, cache_moba_v, state_gla, cache_mem_k, cache_mem_v,
              page_table, mem_prompt, g_mix, g_cross, g_mlp, g_final, w_gla_in, w_gla_g1, w_gla_g2,
              b_gla_g, g_gla_out, w_gla_o, w_moba_qkv, w_moba_o, g_mem, w_mem_kv, w_xq, w_xo,
              w_up, w_down):
    slopes = _alibi_slopes(MOBA_HEADS)
    xp, xs = x_prompt, x_sample
    Bp, Sp, _ = xp.shape
    Bs, Ts, _ = xs.shape
    gla_p, gla_s, mk_p, mv_p, mk_s, mv_s, memk, memv = [], [], [], [], [], [], [], []
    for i in range(DEPTH):
        j = i // N_MIXERS
        hp = _rmsnorm(xp, g_mix[i])
        hs = _rmsnorm(xs, g_mix[i])
        if i % N_MIXERS == 0:
            s0 = jnp.zeros((Bp, GLA_HEADS, GLA_DK, GLA_DV), jnp.float32)
            op, sp = _gla_mixer(hp, s0, w_gla_in[j], w_gla_g1[j], w_gla_g2[j], b_gla_g[j], g_gla_out[j], w_gla_o[j])
            os_, ss = _gla_mixer(hs, state_gla[j], w_gla_in[j], w_gla_g1[j], w_gla_g2[j], b_gla_g[j], g_gla_out[j], w_gla_o[j])
            gla_p.append(sp.astype(xp.dtype))
            gla_s.append(ss.astype(state_gla.dtype))
        else:
            qp, kp, vp = _moba_qkv(hp, w_moba_qkv[j])
            op = _moba_prompt(qp, kp, vp, slopes).reshape(Bp, Sp, MOBA_HEADS * MOBA_HD) @ w_moba_o[j]
            qs, ks, vs = _moba_qkv(hs, w_moba_qkv[j])
            os_ = _moba_sample(qs, ks, vs, cache_moba_k, cache_moba_v, j, page_table, slopes)
            os_ = os_.reshape(Bs, Ts, MOBA_HEADS * MOBA_HD) @ w_moba_o[j]
            mk_p.append(kp)
            mv_p.append(vp)
            mk_s.append(ks)
            mv_s.append(vs)
        xp = xp + op
        xs = xs + os_
        mkp, mvp = _mem_kv(mem_prompt, g_mem[i], w_mem_kv[i])
        memk.append(mkp)
        memv.append(mvp)
        xp = xp + _cross_attn(_rmsnorm(xp, g_cross[i]), mkp, mvp, w_xq[i], w_xo[i])
        xs = xs + _cross_attn(_rmsnorm(xs, g_cross[i]), cache_mem_k[i], cache_mem_v[i], w_xq[i], w_xo[i])
        xp = xp + _sqrelu_mlp(_rmsnorm(xp, g_mlp[i]), w_up[i], w_down[i])
        xs = xs + _sqrelu_mlp(_rmsnorm(xs, g_mlp[i]), w_up[i], w_down[i])
    y_prompt = _rmsnorm(xp, g_final)
    y_sample = _rmsnorm(xs, g_final)
    return (y_prompt, y_sample, jnp.stack(mk_p), jnp.stack(mv_p), jnp.stack(mk_s), jnp.stack(mv_s),
            jnp.stack(gla_p), jnp.stack(gla_s), jnp.stack(memk), jnp.stack(memv))
```

```python
import functools
import math

import numpy as np
import jax
import jax.numpy as jnp
from jax import lax
from jax.experimental import pallas as pl
from jax.experimental.pallas import tpu as pltpu

F32 = jnp.float32
BF16 = jnp.bfloat16

EPS = 1e-6
NEG_INF = -1e30
N_MIXERS = 2
GLA_HEADS = 4
GLA_GATE_TAU = 16.0
GLA_CHUNK = 64
MOBA_HEADS = 8
MOBA_BLOCK = 256
MOBA_TOPK = 3
X_HEADS = 4

LANES = 128
V7X_VMEM_BYTES = 64 << 20
VMEM_CAP_BYTES = V7X_VMEM_BYTES - (8 << 20)


def _cparams(sem, vmem_bytes):
    return pltpu.CompilerParams(dimension_semantics=sem,
                                vmem_limit_bytes=int(min(max(vmem_bytes, 16 << 20), VMEM_CAP_BYTES)))


def _row_tile(rows, pref):
    t = min(rows, pref)
    while rows % t:
        t //= 2
    return t


def _resident(shape, index_map):
    return pl.BlockSpec(shape, index_map, pipeline_mode=pl.Buffered(1))


def _rms(x, g):
    return x * lax.rsqrt(jnp.mean(x * x, axis=-1, keepdims=True) + EPS) * g


def _split3(a):
    a1 = a.astype(BF16).astype(F32)
    r = a - a1
    a2 = r.astype(BF16).astype(F32)
    a3 = (r - a2).astype(BF16).astype(F32)
    return a1, a2, a3


def _dot_nt(a, b):
    return lax.dot_general(a, b, (((1,), (1,)), ((), ())), preferred_element_type=F32)


def _dot_tn(a, b):
    return lax.dot_general(a, b, (((0,), (0,)), ((), ())), preferred_element_type=F32)


def _dot_f32(a, b, nt=False):
    a1, a2, a3 = _split3(a)
    b1, b2, b3 = _split3(b)
    mm = _dot_nt if nt else functools.partial(jnp.dot, preferred_element_type=F32)
    cast = lambda t: t.astype(BF16)
    small = mm(cast(a1), cast(b3)) + mm(cast(a2), cast(b2)) + mm(cast(a3), cast(b1))
    mid = mm(cast(a1), cast(b2)) + mm(cast(a2), cast(b1))
    return mm(cast(a1), cast(b1)) + (mid + small)


def _heads_to_rows(x, heads):
    hd = x.shape[1] // heads
    return jnp.concatenate([x[:, hh * hd:(hh + 1) * hd] for hh in range(heads)], axis=0)


def _rows_to_heads(x, heads):
    t = x.shape[0] // heads
    return jnp.concatenate([x[hh * t:(hh + 1) * t, :] for hh in range(heads)], axis=1)


def _norm_matmul_kernel(x_ref, g_ref, w_ref, *out_refs, splits):
    h = _rms(x_ref[...], g_ref[...]).astype(BF16)
    off = 0
    for o_ref, n in zip(out_refs, splits):
        o_ref[...] = jnp.dot(h, w_ref[:, off:off + n], preferred_element_type=F32)
        off += n


def norm_matmul(x, g, w_bf16, splits, row_tile=512):
    rows, d = x.shape
    n = w_bf16.shape[1]
    assert sum(splits) == n
    tm = _row_tile(rows, row_tile)
    vmem = 2 * tm * d * 4 + d * n * 2 + 2 * tm * n * 4 + 2 * tm * max(splits) * 4 + (4 << 20)
    return pl.pallas_call(
        functools.partial(_norm_matmul_kernel, splits=tuple(splits)),
        grid=(rows // tm,),
        in_specs=[pl.BlockSpec((tm, d), lambda i: (i, 0)),
                  _resident((1, d), lambda i: (0, 0)),
                  _resident((d, n), lambda i: (0, 0))],
        out_specs=[pl.BlockSpec((tm, s), lambda i: (i, 0)) for s in splits],
        out_shape=[jax.ShapeDtypeStruct((rows, s), F32) for s in splits],
        compiler_params=_cparams(("parallel",), vmem),
        name="norm_matmul",
    )(x, g.reshape(1, d), w_bf16)


def _matmul_residual_kernel(x_ref, a_ref, w_ref, o_ref):
    o_ref[...] = x_ref[...] + jnp.dot(a_ref[...].astype(BF16), w_ref[...], preferred_element_type=F32)


def matmul_residual(x, a, w_bf16, row_tile=512):
    rows, d = x.shape
    k = a.shape[1]
    tm = _row_tile(rows, row_tile)
    vmem = 4 * tm * d * 4 + 2 * tm * k * 4 + k * d * 2 + tm * d * 4 + (4 << 20)
    return pl.pallas_call(
        _matmul_residual_kernel,
        grid=(rows // tm,),
        in_specs=[pl.BlockSpec((tm, d), lambda i: (i, 0)),
                  pl.BlockSpec((tm, k), lambda i: (i, 0)),
                  _resident((k, d), lambda i: (0, 0))],
        out_specs=pl.BlockSpec((tm, d), lambda i: (i, 0)),
        out_shape=jax.ShapeDtypeStruct((rows, d), F32),
        compiler_params=_cparams(("parallel",), vmem),
        name="matmul_residual",
    )(x, a, w_bf16)


def _mlp_kernel(x_ref, g_ref, wu_ref, wd_ref, *rest, ff_tile, final_norm):
    if final_norm:
        gf_ref, o_ref = rest
    else:
        (o_ref,) = rest
    x = x_ref[...]
    h = _rms(x, g_ref[...]).astype(BF16)
    d_ff = wu_ref.shape[1]
    acc = x
    for c in range(d_ff // ff_tile):
        a = jnp.dot(h, wu_ref[:, c * ff_tile:(c + 1) * ff_tile], preferred_element_type=F32)
        a = jnp.maximum(a, 0.0)
        a = (a * a).astype(BF16)
        acc = acc + jnp.dot(a, wd_ref[c * ff_tile:(c + 1) * ff_tile, :], preferred_element_type=F32)
    if final_norm:
        acc = _rms(acc, gf_ref[...])
    o_ref[...] = acc


def mlp(x, g, wu_bf16, wd_bf16, g_final=None, row_tile=512, ff_tile=1024):
    rows, d = x.shape
    d_ff = wu_bf16.shape[1]
    tm = _row_tile(rows, row_tile)
    final_norm = g_final is not None
    vmem = 4 * tm * d * 4 + 2 * d * d_ff * 2 + tm * ff_tile * 6 + 2 * tm * d * 4 + (4 << 20)
    in_specs = [pl.BlockSpec((tm, d), lambda i: (i, 0)),
                _resident((1, d), lambda i: (0, 0)),
                _resident((d, d_ff), lambda i: (0, 0)),
                _resident((d_ff, d), lambda i: (0, 0))]
    args = [x, g.reshape(1, d), wu_bf16, wd_bf16]
    if final_norm:
        in_specs.append(_resident((1, d), lambda i: (0, 0)))
        args.append(g_final.reshape(1, d))
    return pl.pallas_call(
        functools.partial(_mlp_kernel, ff_tile=ff_tile, final_norm=final_norm),
        grid=(rows // tm,),
        in_specs=in_specs,
        out_specs=pl.BlockSpec((tm, d), lambda i: (i, 0)),
        out_shape=jax.ShapeDtypeStruct((rows, d), F32),
        compiler_params=_cparams(("parallel",), vmem),
        name="sqrelu_mlp",
    )(*args)


def _softmax_rows(s):
    e = jnp.exp(s - jnp.max(s, axis=-1, keepdims=True))
    return e / jnp.sum(e, axis=-1, keepdims=True)


def _xattn_prompt_kernel(x_ref, g_ref, wq_ref, kv_ref, wo_ref, o_ref, *, heads):
    x = x_ref[...]
    h = _rms(x, g_ref[...]).astype(BF16)
    q = jnp.dot(h, wq_ref[...], preferred_element_type=F32)
    xw = wq_ref.shape[1]
    hd = xw // heads
    kv = kv_ref[...]
    outs = []
    for hh in range(heads):
        qh = q[:, hh * hd:(hh + 1) * hd].astype(BF16)
        kh = kv[:, hh * hd:(hh + 1) * hd].astype(BF16)
        vh = kv[:, xw + hh * hd:xw + (hh + 1) * hd].astype(BF16)
        p = _softmax_rows(_dot_nt(qh, kh) * (hd ** -0.5))
        outs.append(jnp.dot(p.astype(BF16), vh, preferred_element_type=F32).astype(BF16))
    a = jnp.concatenate(outs, axis=1)
    o_ref[...] = x + jnp.dot(a, wo_ref[...], preferred_element_type=F32)


def xattn_prompt(x, g, wq_bf16, memkv, wo_bf16, seq, row_tile=512):
    rows, d = x.shape
    xw = wq_bf16.shape[1]
    mem = memkv.shape[1]
    tm = _row_tile(seq, row_tile)
    per_seq = seq // tm
    vmem = 4 * tm * d * 4 + 2 * d * xw * 2 + 2 * mem * 2 * xw * 4 + 6 * tm * xw * 4 + tm * d * 4 + (4 << 20)
    return pl.pallas_call(
        functools.partial(_xattn_prompt_kernel, heads=X_HEADS),
        grid=(rows // tm,),
        in_specs=[pl.BlockSpec((tm, d), lambda i: (i, 0)),
                  _resident((1, d), lambda i: (0, 0)),
                  _resident((d, xw), lambda i: (0, 0)),
                  pl.BlockSpec((None, mem, 2 * xw), lambda i: (i // per_seq, 0, 0)),
                  _resident((xw, d), lambda i: (0, 0))],
        out_specs=pl.BlockSpec((tm, d), lambda i: (i, 0)),
        out_shape=jax.ShapeDtypeStruct((rows, d), F32),
        compiler_params=_cparams(("parallel",), vmem),
        name="xattn_prompt",
    )(x, g.reshape(1, d), wq_bf16, memkv, wo_bf16)


def _xattn_sample_kernel(q_ref, k_ref, v_ref, o_ref, *, heads):
    group, t, xw = q_ref.shape
    hd = xw // heads
    rows = heads * t
    n_kv = k_ref.shape[1]
    row_head = lax.broadcasted_iota(jnp.int32, (rows, 1), 0) // t
    own_head = (lax.broadcasted_iota(jnp.int32, (rows, n_kv), 1) % heads) == row_head
    for gi in range(group):
        qa = _heads_to_rows(q_ref[gi], heads).astype(BF16)
        s = _dot_nt(qa, k_ref[gi].astype(BF16)) * (hd ** -0.5)
        p = _softmax_rows(jnp.where(own_head, s, NEG_INF))
        o = jnp.dot(p.astype(BF16), v_ref[gi].astype(BF16), preferred_element_type=F32)
        o_ref[gi] = _rows_to_heads(o, heads)


def xattn_sample(q, mem_k, mem_v, layer, group=8):
    bs, t, xw = q.shape
    n_kv, hd = mem_k.shape[2], mem_k.shape[3]
    g = _row_tile(bs, group)
    vmem = 4 * g * n_kv * hd * 4 + 2 * n_kv * hd * 2 + 6 * X_HEADS * t * n_kv * 4 + 4 * g * t * xw * 4 + (4 << 20)
    kv_spec = pl.BlockSpec((None, g, n_kv, hd), lambda i: (layer, i, 0, 0))
    return pl.pallas_call(
        functools.partial(_xattn_sample_kernel, heads=X_HEADS),
        grid=(bs // g,),
        in_specs=[pl.BlockSpec((g, t, xw), lambda i: (i, 0, 0)), kv_spec, kv_spec],
        out_specs=pl.BlockSpec((g, t, xw), lambda i: (i, 0, 0)),
        out_shape=jax.ShapeDtypeStruct((bs, t, xw), F32),
        compiler_params=_cparams(("parallel",), vmem),
        name="xattn_sample",
    )(q, mem_k, mem_v)


def _log_sigmoid(z):
    return -(jnp.maximum(-z, 0.0) + jnp.log1p(jnp.exp(-jnp.abs(z))))


def _gla_inproj_kernel(x_ref, g_ref, w_ref, wg2_ref, bg_ref, y_ref, lf_ref, *, n_main):
    h = _rms(x_ref[...], g_ref[...]).astype(BF16)
    for c in range(0, n_main, 1024):
        y_ref[:, c:c + 1024] = jnp.dot(h, w_ref[:, c:c + 1024], preferred_element_type=F32)
    t = jnp.dot(h, w_ref[:, n_main:], preferred_element_type=F32).astype(BF16)
    z = jnp.dot(t, wg2_ref[...], preferred_element_type=F32) + bg_ref[...]
    lf_ref[...] = _log_sigmoid(z) / GLA_GATE_TAU


def gla_inproj(x, g, w_ext_bf16, wg2_bf16, b_g, row_tile=512):
    rows, d = x.shape
    n_ext = w_ext_bf16.shape[1]
    n_main = n_ext - LANES
    dk_tot = wg2_bf16.shape[1]
    tm = _row_tile(rows, row_tile)
    vmem = 2 * tm * d * 4 + d * n_ext * 2 + 2 * tm * (n_main + dk_tot) * 4 + 2 * tm * 1024 * 4 + (4 << 20)
    return pl.pallas_call(
        functools.partial(_gla_inproj_kernel, n_main=n_main),
        grid=(rows // tm,),
        in_specs=[pl.BlockSpec((tm, d), lambda i: (i, 0)),
                  _resident((1, d), lambda i: (0, 0)),
                  _resident((d, n_ext), lambda i: (0, 0)),
                  _resident((LANES, dk_tot), lambda i: (0, 0)),
                  _resident((1, dk_tot), lambda i: (0, 0))],
        out_specs=[pl.BlockSpec((tm, n_main), lambda i: (i, 0)),
                   pl.BlockSpec((tm, dk_tot), lambda i: (i, 0))],
        out_shape=[jax.ShapeDtypeStruct((rows, n_main), F32),
                   jax.ShapeDtypeStruct((rows, dk_tot), F32)],
        compiler_params=_cparams(("parallel",), vmem),
        name="gla_inproj",
    )(x, g.reshape(1, d), w_ext_bf16, wg2_bf16, b_g.reshape(1, dk_tot))


def _gla_kernel(q_ref, k_ref, v_ref, r_ref, lf_ref, gout_ref, *rest, chunk, n_sub, heads, has_s0, aliased):
    rest = list(rest)
    s0_ref = rest.pop(0) if has_s0 else None
    if aliased:
        rest.pop(0)
    o_ref, s_ref = rest
    dk = q_ref.shape[1] // heads
    dv = v_ref.shape[1] // heads

    @pl.when(pl.program_id(1) == 0)
    def _():
        if has_s0:
            s_ref[...] = s0_ref[...]
        else:
            s_ref[...] = jnp.zeros(s_ref.shape, F32)

    op = (lambda a: a.astype(BF16)) if chunk >= 16 else (lambda a: a.astype(BF16).astype(F32))
    row = lax.broadcasted_iota(jnp.int32, (chunk, chunk), 0)
    col = lax.broadcasted_iota(jnp.int32, (chunk, chunk), 1)
    tril = row >= col
    ltri = op(jnp.where(tril, 1.0, 0.0).astype(F32))
    eye = (lax.broadcasted_iota(jnp.int32, (dk, dk), 0) == lax.broadcasted_iota(jnp.int32, (dk, dk), 1))
    mm = functools.partial(jnp.dot, preferred_element_type=F32)

    for u in range(n_sub):
        rs = slice(u * chunk, (u + 1) * chunk)
        g1, g2, g3 = _split3(lf_ref[rs, :])
        gcum = mm(ltri, op(g1)) + (mm(ltri, op(g2)) + mm(ltri, op(g3)))
        for hh in range(heads):
            ks = slice(hh * dk, (hh + 1) * dk)
            vs = slice(hh * dv, (hh + 1) * dv)
            gh = gcum[:, ks]
            qh = q_ref[rs, ks] * (dk ** -0.5)
            kh = k_ref[rs, ks]
            vh = v_ref[rs, vs]
            g_last = gh[chunk - 1:chunk, :]
            q_dec = op(qh * jnp.exp(gh))
            k_dec = op(kh * jnp.exp(-gh))
            k_rem = op(kh * jnp.exp(g_last - gh))
            vb = op(vh)
            s_old = s_ref[hh]
            a = jnp.where(tril, _dot_nt(q_dec, k_dec), 0.0)
            o = mm(q_dec, op(s_old)) + mm(op(a), vb)
            decay = jnp.sum(jnp.where(eye, jnp.broadcast_to(jnp.exp(g_last), (dk, dk)), 0.0),
                            axis=1, keepdims=True)
            s_ref[hh] = decay * s_old + _dot_tn(k_rem, vb)
            on = _rms(o, gout_ref[:, vs])
            rh = r_ref[rs, vs]
            o_ref[rs, vs] = on * (rh * jax.nn.sigmoid(rh))


def gla_core(y, logf, g_out, s0_all, layer, states, n_layers, batch, seq, row_tile=256):
    rows, n = y.shape
    dk_tot = logf.shape[1]
    dv_tot = (n - 2 * dk_tot) // 2
    assert dv_tot == 2 * dk_tot
    heads = GLA_HEADS
    dk, dv = dk_tot // heads, dv_tot // heads
    chunk = math.gcd(seq, GLA_CHUNK)
    tl = _row_tile(seq, max(row_tile, chunk))
    n_sub = tl // chunk
    per_seq = seq // tl
    has_s0 = s0_all is not None
    aliased = states is not None
    state_spec = pl.BlockSpec((None, None, heads, dk, dv), lambda b, c: (layer, b, 0, 0, 0))
    in_specs = [pl.BlockSpec((tl, dk_tot), lambda b, c: (b * per_seq + c, 0)),
                pl.BlockSpec((tl, dk_tot), lambda b, c: (b * per_seq + c, 1)),
                pl.BlockSpec((tl, dv_tot), lambda b, c: (b * per_seq + c, 1)),
                pl.BlockSpec((tl, dv_tot), lambda b, c: (b * per_seq + c, 2)),
                pl.BlockSpec((tl, dk_tot), lambda b, c: (b * per_seq + c, 0)),
                _resident((1, dv_tot), lambda b, c: (0, 0))]
    args = [y, y, y, y, logf, g_out.reshape(1, dv_tot)]
    if has_s0:
        in_specs.append(state_spec)
        args.append(s0_all)
    aliases = {}
    if aliased:
        aliases = {len(args): 1}
        in_specs.append(pl.BlockSpec(memory_space=pl.ANY))
        args.append(states)
    state_bytes = heads * dk * dv * 4
    vmem = 2 * tl * (3 * dk_tot + 2 * dv_tot) * 4 + 2 * tl * dv_tot * 4 + 4 * state_bytes + (16 << 20)
    return pl.pallas_call(
        functools.partial(_gla_kernel, chunk=chunk, n_sub=n_sub, heads=heads, has_s0=has_s0, aliased=aliased),
        grid=(batch, per_seq),
        in_specs=in_specs,
        out_specs=[pl.BlockSpec((tl, dv_tot), lambda b, c: (b * per_seq + c, 0)), state_spec],
        out_shape=[jax.ShapeDtypeStruct((rows, dv_tot), F32),
                   jax.ShapeDtypeStruct((n_layers, batch, heads, dk, dv), F32)],
        input_output_aliases=aliases,
        compiler_params=_cparams(("parallel", "arbitrary"), vmem),
        name="gla_core",
    )(*args)


def _alibi_slopes(n):
    return np.asarray(2.0 ** (-8.0 * np.arange(1, n + 1) / n), dtype=np.float32)


def _moba_qkv_kernel(x_ref, g_ref, w_ref, *rest, heads, dense_kv, aliased):
    rest = list(rest)
    if aliased:
        del rest[:2]
    q_ref = rest.pop(0)
    if dense_kv:
        k_ref, v_ref = rest.pop(0), rest.pop(0)
    kn_ref, vn_ref = rest
    tm = x_ref.shape[0]
    width = q_ref.shape[1]
    hd = width // heads
    h = _rms(x_ref[...], g_ref[...]).astype(BF16)
    q_ref[...] = jnp.dot(h, w_ref[:, :width], preferred_element_type=F32)
    k = jnp.dot(h, w_ref[:, width:2 * width], preferred_element_type=F32)
    v = jnp.dot(h, w_ref[:, 2 * width:], preferred_element_type=F32)
    if dense_kv:
        k_ref[...] = k
        v_ref[...] = v
    for hh in range(heads):
        kn_ref[pl.ds(hh, tm, stride=heads), :] = k[:, hh * hd:(hh + 1) * hd]
        vn_ref[pl.ds(hh, tm, stride=heads), :] = v[:, hh * hd:(hh + 1) * hd]


def moba_qkv(x, g, w_bf16, layer, kv_rows, n_layers, dense_kv, row_tile=512):
    rows, d = x.shape
    width = w_bf16.shape[1] // 3
    heads = MOBA_HEADS
    hd = width // heads
    tm = _row_tile(rows, row_tile)
    aliased = kv_rows is not None
    n_dense = 3 if dense_kv else 1
    in_specs = [pl.BlockSpec((tm, d), lambda i: (i, 0)),
                _resident((1, d), lambda i: (0, 0)),
                _resident((d, 3 * width), lambda i: (0, 0))]
    args = [x, g.reshape(1, d), w_bf16]
    aliases = {}
    if aliased:
        aliases = {3: n_dense, 4: n_dense + 1}
        in_specs += [pl.BlockSpec(memory_space=pl.ANY)] * 2
        args += list(kv_rows)
    dense_spec = pl.BlockSpec((tm, width), lambda i: (i, 0))
    rows_spec = pl.BlockSpec((None, tm * heads, hd), lambda i: (layer, i, 0))
    vmem = 2 * tm * d * 4 + d * 3 * width * 2 + (2 * n_dense + 4 + 3) * tm * width * 4 + (4 << 20)
    return pl.pallas_call(
        functools.partial(_moba_qkv_kernel, heads=heads, dense_kv=dense_kv, aliased=aliased),
        grid=(rows // tm,),
        in_specs=in_specs,
        out_specs=[dense_spec] * n_dense + [rows_spec] * 2,
        out_shape=[jax.ShapeDtypeStruct((rows, width), F32)] * n_dense
                  + [jax.ShapeDtypeStruct((n_layers, rows * heads, hd), F32)] * 2,
        input_output_aliases=aliases,
        compiler_params=_cparams(("parallel",), vmem),
        name="moba_qkv",
    )(*args)


def _topk_mask(gate, block_ids, valid, topk, n_blocks, axis):
    gm = jnp.where(valid, gate, NEG_INF)
    rank = jnp.zeros(gate.shape, F32)
    for m in range(n_blocks):
        gsel = gm[m:m + 1, :] if axis == 0 else gm[:, m:m + 1]
        ahead = jnp.where(gsel > gm, 1.0, jnp.where(gsel == gm, jnp.where(block_ids > m, 1.0, 0.0), 0.0))
        rank = rank + ahead
    return jnp.where(valid, jnp.where(rank < topk, 1.0, 0.0), 0.0)


def _moba_prompt_kernel(slopes_ref, q_ref, k_ref, v_ref, o_ref, kb_ref, vt_ref, km_ref, sel_ref,
                        *, n_blocks, topk):
    blk = MOBA_BLOCK
    hd = q_ref.shape[1]
    scale = hd ** -0.5
    i = pl.program_id(2)
    slope = slopes_ref[pl.program_id(1)]

    @pl.when(i == 0)
    def _():
        km_ref[...] = jnp.zeros(km_ref.shape, F32)
        for n in range(n_blocks):
            kblk = k_ref[n * blk:(n + 1) * blk, :]
            kb_ref[n * blk:(n + 1) * blk, :] = kblk.astype(BF16)
            km_ref[n:n + 1, :] = jnp.mean(kblk, axis=0, keepdims=True)
            vt_ref[n] = v_ref[n * blk:(n + 1) * blk, :].T.astype(BF16)

    q_t = q_ref[...].T
    q_tb = q_t.astype(BF16)
    nbp = km_ref.shape[0]
    gate = _dot_f32(km_ref[...], q_t)
    block_ids = lax.broadcasted_iota(jnp.int32, (nbp, blk), 0)
    sel_ref[...] = _topk_mask(gate, block_ids, block_ids < i, topk, n_blocks, axis=0)

    dpos = (lax.broadcasted_iota(jnp.int32, (blk, blk), 1)
            - lax.broadcasted_iota(jnp.int32, (blk, blk), 0)).astype(F32)
    mm = functools.partial(jnp.dot, preferred_element_type=F32)

    k_own = kb_ref[pl.ds(pl.multiple_of(i * blk, blk), blk), :]
    s = mm(k_own, q_tb) * scale - slope * dpos
    s = jnp.where(dpos >= 0, s, NEG_INF)
    m0 = jnp.max(s, axis=0, keepdims=True)
    p = jnp.exp(s - m0)
    l0 = jnp.sum(p, axis=0, keepdims=True)
    acc0 = mm(vt_ref[i], p.astype(BF16))

    def body(j, carry):
        m, l, acc = carry
        kj = kb_ref[pl.ds(pl.multiple_of(j * blk, blk), blk), :]
        d = dpos + ((i - j) * blk).astype(F32)
        s = mm(kj, q_tb) * scale - slope * d
        s = jnp.where(sel_ref[pl.ds(j, 1), :] > 0.5, s, NEG_INF)
        m_new = jnp.maximum(m, jnp.max(s, axis=0, keepdims=True))
        alpha = jnp.exp(m - m_new)
        p = jnp.exp(s - m_new)
        l = alpha * l + jnp.sum(p, axis=0, keepdims=True)
        acc = alpha * acc + mm(vt_ref[j], p.astype(BF16))
        return m_new, l, acc

    _, l, acc = lax.fori_loop(0, i, body, (m0, l0, acc0))
    o_ref[...] = (acc / l).T


def moba_prompt(q, k, v, batch, seq):
    rows, width = q.shape
    heads = MOBA_HEADS
    hd = width // heads
    blk = MOBA_BLOCK
    assert seq % blk == 0 and hd == LANES
    n_blocks = seq // blk
    topk = min(MOBA_TOPK, n_blocks - 1)
    nbp = -(-n_blocks // 16) * 16
    slopes = jnp.asarray(_alibi_slopes(heads))
    grid_spec = pltpu.PrefetchScalarGridSpec(
        num_scalar_prefetch=1,
        grid=(batch, heads, n_blocks),
        in_specs=[pl.BlockSpec((blk, hd), lambda b, h, i, sl: (b * n_blocks + i, h)),
                  pl.BlockSpec((seq, hd), lambda b, h, i, sl: (b, h)),
                  pl.BlockSpec((seq, hd), lambda b, h, i, sl: (b, h))],
        out_specs=pl.BlockSpec((blk, hd), lambda b, h, i, sl: (b * n_blocks + i, h)),
        scratch_shapes=[pltpu.VMEM((seq, hd), BF16),
                        pltpu.VMEM((n_blocks, hd, blk), BF16),
                        pltpu.VMEM((nbp, hd), F32),
                        pltpu.VMEM((nbp, blk), F32)])
    vmem = 4 * seq * hd * 4 + 2 * seq * hd * 2 + 8 * blk * blk * 4 + (8 << 20)
    return pl.pallas_call(
        functools.partial(_moba_prompt_kernel, n_blocks=n_blocks, topk=topk),
        grid_spec=grid_spec,
        out_shape=jax.ShapeDtypeStruct((rows, width), F32),
        compiler_params=_cparams(("parallel", "parallel", "arbitrary"), vmem),
        name="moba_prompt",
    )(slopes, q, k, v)


def _moba_sample_kernel(pt_ref, q_ref, kn_ref, vn_ref, slope_ref, *rest, n_blocks, ppb, topk, heads, past_len):
    k_refs = rest[:ppb]
    v_refs = rest[ppb:2 * ppb]
    o_ref, qa_ref, km_ref, m_ref, l_ref, on_ref = rest[2 * ppb:]
    n = pl.program_id(1)
    t_new, width = q_ref.shape
    hd = width // heads
    rows = heads * t_new
    page = k_refs[0].shape[0]
    cols = page * heads
    scale = hd ** -0.5
    row_iota = lax.broadcasted_iota(jnp.int32, (rows, 1), 0)
    row_head = row_iota // t_new
    row_t = row_iota % t_new
    mm = functools.partial(jnp.dot, preferred_element_type=F32)

    @pl.when(n == 0)
    def _():
        qa_ref[...] = _heads_to_rows(q_ref[...], heads)
        km_ref[...] = jnp.zeros(km_ref.shape, F32)

    qa = qa_ref[...]
    qab = qa.astype(BF16)
    slope = slope_ref[:, 0:1]
    q_pos = (past_len + row_t).astype(F32)
    lane = lax.broadcasted_iota(jnp.int32, (rows, cols), 1)
    own_head = (lane % heads) == row_head
    lane_key = lane // heads

    km = jnp.zeros((heads, hd), F32)
    for pp in range(ppb):
        kp = k_refs[pp][...]
        km = km + jnp.mean(kp, axis=0)
        kb = kp.reshape(cols, hd).astype(BF16)
        vb = v_refs[pp][...].reshape(cols, hd).astype(BF16)
        key_pos = ((n * ppb + pp) * page + lane_key).astype(F32)
        s = _dot_nt(qab, kb) * scale - slope * (q_pos - key_pos)
        s = jnp.where(own_head, s, NEG_INF)
        m = jnp.max(s, axis=1, keepdims=True)
        p = jnp.exp(s - m)
        idx = n * ppb + pp
        m_ref[idx] = jnp.broadcast_to(m, (rows, LANES))
        l_ref[idx] = jnp.broadcast_to(jnp.sum(p, axis=1, keepdims=True), (rows, LANES))
        on_ref[idx] = mm(p.astype(BF16), vb)
    km_ref[pl.ds(pl.multiple_of(n * heads, heads), heads), :] = km / ppb

    @pl.when(n == n_blocks - 1)
    def _():
        lane_g = lax.broadcasted_iota(jnp.int32, (rows, LANES), 1)
        gate = _dot_f32(qa, km_ref[...], nt=True)
        valid = ((lane_g % heads) == row_head) & (lane_g < n_blocks * heads)
        sel = _topk_mask(gate, lane_g, valid, topk, n_blocks * heads, axis=1)
        sel_blk = [jnp.sum(sel[:, b * heads:(b + 1) * heads], axis=1, keepdims=True) > 0.5
                   for b in range(n_blocks)]
        pad = jnp.zeros((LANES - t_new * heads, hd), F32)
        k_own = jnp.concatenate([kn_ref[...], pad], axis=0).astype(BF16)
        v_own = jnp.concatenate([vn_ref[...], pad], axis=0).astype(BF16)
        d_own = (row_t - lane_g // heads).astype(F32)
        s_own = _dot_nt(qab, k_own) * scale - slope * d_own
        s_own = jnp.where(((lane_g % heads) == row_head) & (d_own >= 0), s_own, NEG_INF)
        m_fin = jnp.max(s_own, axis=1, keepdims=True)
        for idx in range(n_blocks * ppb):
            m_fin = jnp.maximum(m_fin, jnp.where(sel_blk[idx // ppb], m_ref[idx][:, 0:1], NEG_INF))
        p_own = jnp.exp(s_own - m_fin)
        l_fin = jnp.sum(p_own, axis=1, keepdims=True)
        acc = mm(p_own.astype(BF16), v_own)
        for idx in range(n_blocks * ppb):
            w = jnp.where(sel_blk[idx // ppb], jnp.exp(m_ref[idx][:, 0:1] - m_fin), 0.0)
            l_fin = l_fin + w * l_ref[idx][:, 0:1]
            acc = acc + w * on_ref[idx]
        o_ref[...] = _rows_to_heads(acc / l_fin, heads)


def moba_sample(q, kv_rows, cache_k, cache_v, layer, page_table):
    bs, t_new, width = q.shape
    heads = MOBA_HEADS
    hd = width // heads
    page = cache_k.shape[2]
    n_pages = page_table.shape[1]
    past_len = n_pages * page
    ppb = MOBA_BLOCK // page
    n_blocks = past_len // MOBA_BLOCK
    assert n_blocks * ppb == n_pages and n_blocks >= 1
    assert t_new * heads <= LANES and n_blocks * heads <= LANES and heads % 8 == 0
    topk = min(MOBA_TOPK, n_blocks)
    rows = heads * t_new
    slope_rows = jnp.asarray(np.repeat(_alibi_slopes(heads), t_new)[:, None] * np.ones((1, LANES), np.float32))

    def paged(pp):
        return pl.BlockSpec((None, None, page, heads, hd),
                            lambda b, n, pt: (layer, pt[b, n * ppb + pp], 0, 0, 0))

    new_spec = pl.BlockSpec((None, rows, hd), lambda b, n, pt: (layer, b, 0))
    seq_spec = pl.BlockSpec((None, t_new, width), lambda b, n, pt: (b, 0, 0))
    grid_spec = pltpu.PrefetchScalarGridSpec(
        num_scalar_prefetch=1,
        grid=(bs, n_blocks),
        in_specs=[seq_spec, new_spec, new_spec,
                  pl.BlockSpec((rows, LANES), lambda b, n, pt: (0, 0))]
                 + [paged(pp) for pp in range(ppb)] + [paged(pp) for pp in range(ppb)],
        out_specs=seq_spec,
        scratch_shapes=[pltpu.VMEM((rows, hd), F32),
                        pltpu.VMEM((LANES, hd), F32),
                        pltpu.VMEM((n_pages, rows, LANES), F32),
                        pltpu.VMEM((n_pages, rows, LANES), F32),
                        pltpu.VMEM((n_pages, rows, hd), F32)])
    vmem = 4 * ppb * page * width * 4 + 2 * page * width * 2 + 8 * rows * page * heads * 4 \
        + 3 * n_pages * rows * LANES * 4 + (8 << 20)
    return pl.pallas_call(
        functools.partial(_moba_sample_kernel, n_blocks=n_blocks, ppb=ppb, topk=topk, heads=heads,
                          past_len=past_len),
        grid_spec=grid_spec,
        out_shape=jax.ShapeDtypeStruct((bs, t_new, width), F32),
        compiler_params=_cparams(("parallel", "arbitrary"), vmem),
        name="moba_sample",
    )(page_table, q, kv_rows[0], kv_rows[1], slope_rows, *([cache_k] * ppb), *([cache_v] * ppb))


def kernel(x_prompt, x_sample, cache_moba_k, cache_moba_v, state_gla, cache_mem_k, cache_mem_v, page_table, mem_prompt, g_mix, g_cross, g_mlp, g_final, w_gla_in, w_gla_g1, w_gla_g2, b_gla_g, g_gla_out, w_gla_o, w_moba_qkv, w_moba_o, g_mem, w_mem_kv, w_xq, w_xo, w_up, w_down):
    bp, sp, d = x_prompt.shape
    bs, ts, _ = x_sample.shape
    depth = g_mix.shape[0]
    xp = x_prompt.reshape(bp * sp, d)
    xs = x_sample.reshape(bs * ts, d)
    bf = lambda w: w.astype(BF16)

    n_gla = state_gla.shape[0]
    n_moba, _, _, m_heads, m_hd = cache_moba_k.shape
    m_width = m_heads * m_hd
    mem_len = mem_prompt.shape[1]
    mem_rows = mem_prompt.reshape(bp * mem_len, d)
    x_hd = cache_mem_k.shape[4]
    xw = w_xq.shape[2]
    mem_k_rows = cache_mem_k.reshape(depth, bs, mem_len * X_HEADS, x_hd)
    mem_v_rows = cache_mem_v.reshape(depth, bs, mem_len * X_HEADS, x_hd)
    rank = w_gla_g1.shape[2]

    gla_p = gla_s = kv_p = kv_s = None
    memk, memv = [], []
    for i in range(depth):
        j = i // N_MIXERS
        if i % N_MIXERS == 0:
            w_ext = bf(jnp.concatenate([w_gla_in[j], jnp.pad(w_gla_g1[j], ((0, 0), (0, LANES - rank)))], axis=1))
            wg2 = bf(jnp.pad(w_gla_g2[j], ((0, LANES - rank), (0, 0))))
            wo = bf(w_gla_o[j])
            yp, lfp = gla_inproj(xp, g_mix[i], w_ext, wg2, b_gla_g[j])
            ogp, gla_p = gla_core(yp, lfp, g_gla_out[j], None, j, gla_p, n_gla, bp, sp)
            xp = matmul_residual(xp, ogp, wo)
            ys, lfs = gla_inproj(xs, g_mix[i], w_ext, wg2, b_gla_g[j])
            ogs, gla_s = gla_core(ys, lfs, g_gla_out[j], state_gla, j, gla_s, n_gla, bs, ts)
            xs = matmul_residual(xs, ogs, wo)
        else:
            wqkv = bf(w_moba_qkv[j])
            wo = bf(w_moba_o[j])
            qp, kp, vp, *kv_p = moba_qkv(xp, g_mix[i], wqkv, j, kv_p, n_moba, dense_kv=True)
            ap = moba_prompt(qp, kp, vp, bp, sp)
            xp = matmul_residual(xp, ap, wo)
            qs, *kv_s = moba_qkv(xs, g_mix[i], wqkv, j, kv_s, n_moba, dense_kv=False)
            a_s = moba_sample(qs.reshape(bs, ts, m_width), kv_s, cache_moba_k, cache_moba_v, j, page_table)
            xs = matmul_residual(xs, a_s.reshape(bs * ts, m_width), wo)
        (memkv,) = norm_matmul(mem_rows, g_mem[i], bf(w_mem_kv[i]), (2 * xw,))
        memkv = memkv.reshape(bp, mem_len, 2 * xw)
        memk.append(memkv[:, :, :xw].reshape(bp, mem_len, X_HEADS, x_hd))
        memv.append(memkv[:, :, xw:].reshape(bp, mem_len, X_HEADS, x_hd))
        wq = bf(w_xq[i])
        wxo = bf(w_xo[i])
        xp = xattn_prompt(xp, g_cross[i], wq, memkv, wxo, sp)
        (qx,) = norm_matmul(xs, g_cross[i], wq, (xw,))
        ax = xattn_sample(qx.reshape(bs, ts, xw), mem_k_rows, mem_v_rows, i)
        xs = matmul_residual(xs, ax.reshape(bs * ts, xw), wxo)
        gf = g_final if i == depth - 1 else None
        wu, wd = bf(w_up[i]), bf(w_down[i])
        xp = mlp(xp, g_mlp[i], wu, wd, gf)
        xs = mlp(xs, g_mlp[i], wu, wd, gf)
    return (xp.reshape(bp, sp, d), xs.reshape(bs, ts, d),
            kv_p[0].reshape(n_moba, bp, sp, m_heads, m_hd), kv_p[1].reshape(n_moba, bp, sp, m_heads, m_hd),
            kv_s[0].reshape(n_moba, bs, ts, m_heads, m_hd), kv_s[1].reshape(n_moba, bs, ts, m_heads, m_hd),
            gla_p, gla_s, jnp.stack(memk), jnp.stack(memv))
```

```python
import functools
import math

import numpy as np
import jax
import jax.numpy as jnp
from jax import lax
from jax.experimental import pallas as pl
from jax.experimental.pallas import tpu as pltpu

F32 = jnp.float32
BF16 = jnp.bfloat16

EPS = 1e-6
NEG_INF = -1e30
N_MIXERS = 2
GLA_HEADS = 4
GLA_GATE_TAU = 16.0
GLA_CHUNK = 64
MOBA_HEADS = 8
MOBA_BLOCK = 256
MOBA_TOPK = 3
X_HEADS = 4

LANES = 128
V7X_VMEM_BYTES = 64 << 20
VMEM_CAP_BYTES = V7X_VMEM_BYTES - (8 << 20)


def _cparams(sem, vmem_bytes):
    return pltpu.CompilerParams(dimension_semantics=sem,
                                vmem_limit_bytes=int(min(max(vmem_bytes, 16 << 20), VMEM_CAP_BYTES)))


def _row_tile(rows, pref):
    t = min(rows, pref)
    while rows % t:
        t //= 2
    return t


def _resident(shape, index_map):
    return pl.BlockSpec(shape, index_map, pipeline_mode=pl.Buffered(1))


def _rms(x, g):
    return x * lax.rsqrt(jnp.mean(x * x, axis=-1, keepdims=True) + EPS) * g


def _split3(a):
    a1 = a.astype(BF16).astype(F32)
    r = a - a1
    a2 = r.astype(BF16).astype(F32)
    a3 = (r - a2).astype(BF16).astype(F32)
    return a1, a2, a3


def _dot_nt(a, b):
    return lax.dot_general(a, b, (((1,), (1,)), ((), ())), preferred_element_type=F32)


def _dot_tn(a, b):
    return lax.dot_general(a, b, (((0,), (0,)), ((), ())), preferred_element_type=F32)


def _dot_f32(a, b, nt=False):
    a1, a2, a3 = _split3(a)
    b1, b2, b3 = _split3(b)
    mm = _dot_nt if nt else functools.partial(jnp.dot, preferred_element_type=F32)
    cast = lambda t: t.astype(BF16)
    small = mm(cast(a1), cast(b3)) + mm(cast(a2), cast(b2)) + mm(cast(a3), cast(b1))
    mid = mm(cast(a1), cast(b2)) + mm(cast(a2), cast(b1))
    return mm(cast(a1), cast(b1)) + (mid + small)


def _heads_to_rows(x, heads):
    hd = x.shape[1] // heads
    return jnp.concatenate([x[:, hh * hd:(hh + 1) * hd] for hh in range(heads)], axis=0)


def _rows_to_heads(x, heads):
    t = x.shape[0] // heads
    return jnp.concatenate([x[hh * t:(hh + 1) * t, :] for hh in range(heads)], axis=1)


def _norm_matmul_kernel(x_ref, g_ref, w_ref, *out_refs, splits):
    h = _rms(x_ref[...], g_ref[...]).astype(BF16)
    off = 0
    for o_ref, n in zip(out_refs, splits):
        o_ref[...] = jnp.dot(h, w_ref[:, off:off + n], preferred_element_type=F32)
        off += n


def norm_matmul(x, g, w_bf16, splits, row_tile=512):
    rows, d = x.shape
    n = w_bf16.shape[1]
    assert sum(splits) == n
    tm = _row_tile(rows, row_tile)
    vmem = 2 * tm * d * 4 + d * n * 2 + 2 * tm * n * 4 + 2 * tm * max(splits) * 4 + (4 << 20)
    return pl.pallas_call(
        functools.partial(_norm_matmul_kernel, splits=tuple(splits)),
        grid=(rows // tm,),
        in_specs=[pl.BlockSpec((tm, d), lambda i: (i, 0)),
                  _resident((1, d), lambda i: (0, 0)),
                  _resident((d, n), lambda i: (0, 0))],
        out_specs=[pl.BlockSpec((tm, s), lambda i: (i, 0)) for s in splits],
        out_shape=[jax.ShapeDtypeStruct((rows, s), F32) for s in splits],
        compiler_params=_cparams(("parallel",), vmem),
        name="norm_matmul",
    )(x, g.reshape(1, d), w_bf16)


def _matmul_residual_kernel(x_ref, a_ref, w_ref, o_ref):
    o_ref[...] = x_ref[...] + jnp.dot(a_ref[...].astype(BF16), w_ref[...], preferred_element_type=F32)


def matmul_residual(x, a, w_bf16, row_tile=512):
    rows, d = x.shape
    k = a.shape[1]
    tm = _row_tile(rows, row_tile)
    vmem = 4 * tm * d * 4 + 2 * tm * k * 4 + k * d * 2 + tm * d * 4 + (4 << 20)
    return pl.pallas_call(
        _matmul_residual_kernel,
        grid=(rows // tm,),
        in_specs=[pl.BlockSpec((tm, d), lambda i: (i, 0)),
                  pl.BlockSpec((tm, k), lambda i: (i, 0)),
                  _resident((k, d), lambda i: (0, 0))],
        out_specs=pl.BlockSpec((tm, d), lambda i: (i, 0)),
        out_shape=jax.ShapeDtypeStruct((rows, d), F32),
        compiler_params=_cparams(("parallel",), vmem),
        name="matmul_residual",
    )(x, a, w_bf16)


def _mlp_kernel(x_ref, g_ref, wu_ref, wd_ref, *rest, ff_tile, final_norm):
    if final_norm:
        gf_ref, o_ref = rest
    else:
        (o_ref,) = rest
    x = x_ref[...]
    h = _rms(x, g_ref[...]).astype(BF16)
    d_ff = wu_ref.shape[1]
    acc = x
    for c in range(d_ff // ff_tile):
        a = jnp.dot(h, wu_ref[:, c * ff_tile:(c + 1) * ff_tile], preferred_element_type=F32)
        a = jnp.maximum(a, 0.0)
        a = (a * a).astype(BF16)
        acc = acc + jnp.dot(a, wd_ref[c * ff_tile:(c + 1) * ff_tile, :], preferred_element_type=F32)
    if final_norm:
        acc = _rms(acc, gf_ref[...])
    o_ref[...] = acc


def mlp(x, g, wu_bf16, wd_bf16, g_final=None, row_tile=512, ff_tile=1024):
    rows, d = x.shape
    d_ff = wu_bf16.shape[1]
    tm = _row_tile(rows, row_tile)
    final_norm = g_final is not None
    vmem = 4 * tm * d * 4 + 2 * d * d_ff * 2 + tm * ff_tile * 6 + 2 * tm * d * 4 + (4 << 20)
    in_specs = [pl.BlockSpec((tm, d), lambda i: (i, 0)),
                _resident((1, d), lambda i: (0, 0)),
                _resident((d, d_ff), lambda i: (0, 0)),
                _resident((d_ff, d), lambda i: (0, 0))]
    args = [x, g.reshape(1, d), wu_bf16, wd_bf16]
    if final_norm:
        in_specs.append(_resident((1, d), lambda i: (0, 0)))
        args.append(g_final.reshape(1, d))
    return pl.pallas_call(
        functools.partial(_mlp_kernel, ff_tile=ff_tile, final_norm=final_norm),
        grid=(rows // tm,),
        in_specs=in_specs,
        out_specs=pl.BlockSpec((tm, d), lambda i: (i, 0)),
        out_shape=jax.ShapeDtypeStruct((rows, d), F32),
        compiler_params=_cparams(("parallel",), vmem),
        name="sqrelu_mlp",
    )(*args)


def _softmax_rows(s):
    e = jnp.exp(s - jnp.max(s, axis=-1, keepdims=True))
    return e / jnp.sum(e, axis=-1, keepdims=True)


def _xattn_prompt_kernel(x_ref, g_ref, wq_ref, kv_ref, wo_ref, o_ref, *, heads):
    x = x_ref[...]
    h = _rms(x, g_ref[...]).astype(BF16)
    q = jnp.dot(h, wq_ref[...], preferred_element_type=F32)
    xw = wq_ref.shape[1]
    hd = xw // heads
    kv = kv_ref[...]
    outs = []
    for hh in range(heads):
        qh = q[:, hh * hd:(hh + 1) * hd].astype(BF16)
        kh = kv[:, hh * hd:(hh + 1) * hd].astype(BF16)
        vh = kv[:, xw + hh * hd:xw + (hh + 1) * hd].astype(BF16)
        p = _softmax_rows(_dot_nt(qh, kh) * (hd ** -0.5))
        outs.append(jnp.dot(p.astype(BF16), vh, preferred_element_type=F32).astype(BF16))
    a = jnp.concatenate(outs, axis=1)
    o_ref[...] = x + jnp.dot(a, wo_ref[...], preferred_element_type=F32)


def xattn_prompt(x, g, wq_bf16, memkv, wo_bf16, seq, row_tile=512):
    rows, d = x.shape
    xw = wq_bf16.shape[1]
    mem = memkv.shape[1]
    tm = _row_tile(seq, row_tile)
    per_seq = seq // tm
    vmem = 4 * tm * d * 4 + 2 * d * xw * 2 + 2 * mem * 2 * xw * 4 + 6 * tm * xw * 4 + tm * d * 4 + (4 << 20)
    return pl.pallas_call(
        functools.partial(_xattn_prompt_kernel, heads=X_HEADS),
        grid=(rows // tm,),
        in_specs=[pl.BlockSpec((tm, d), lambda i: (i, 0)),
                  _resident((1, d), lambda i: (0, 0)),
                  _resident((d, xw), lambda i: (0, 0)),
                  pl.BlockSpec((None, mem, 2 * xw), lambda i: (i // per_seq, 0, 0)),
                  _resident((xw, d), lambda i: (0, 0))],
        out_specs=pl.BlockSpec((tm, d), lambda i: (i, 0)),
        out_shape=jax.ShapeDtypeStruct((rows, d), F32),
        compiler_params=_cparams(("parallel",), vmem),
        name="xattn_prompt",
    )(x, g.reshape(1, d), wq_bf16, memkv, wo_bf16)


def _xattn_sample_kernel(q_ref, k_ref, v_ref, o_ref, *, heads):
    group, t, xw = q_ref.shape
    hd = xw // heads
    rows = heads * t
    n_kv = k_ref.shape[1]
    row_head = lax.broadcasted_iota(jnp.int32, (rows, 1), 0) // t
    own_head = (lax.broadcasted_iota(jnp.int32, (rows, n_kv), 1) % heads) == row_head
    for gi in range(group):
        qa = _heads_to_rows(q_ref[gi], heads).astype(BF16)
        s = _dot_nt(qa, k_ref[gi].astype(BF16)) * (hd ** -0.5)
        p = _softmax_rows(jnp.where(own_head, s, NEG_INF))
        o = jnp.dot(p.astype(BF16), v_ref[gi].astype(BF16), preferred_element_type=F32)
        o_ref[gi] = _rows_to_heads(o, heads)


def xattn_sample(q, mem_k, mem_v, layer, group=8):
    bs, t, xw = q.shape
    n_kv, hd = mem_k.shape[2], mem_k.shape[3]
    g = _row_tile(bs, group)
    vmem = 4 * g * n_kv * hd * 4 + 2 * n_kv * hd * 2 + 6 * X_HEADS * t * n_kv * 4 + 4 * g * t * xw * 4 + (4 << 20)
    kv_spec = pl.BlockSpec((None, g, n_kv, hd), lambda i: (layer, i, 0, 0))
    return pl.pallas_call(
        functools.partial(_xattn_sample_kernel, heads=X_HEADS),
        grid=(bs // g,),
        in_specs=[pl.BlockSpec((g, t, xw), lambda i: (i, 0, 0)), kv_spec, kv_spec],
        out_specs=pl.BlockSpec((g, t, xw), lambda i: (i, 0, 0)),
        out_shape=jax.ShapeDtypeStruct((bs, t, xw), F32),
        compiler_params=_cparams(("parallel",), vmem),
        name="xattn_sample",
    )(q, mem_k, mem_v)


def _log_sigmoid(z):
    return -(jnp.maximum(-z, 0.0) + jnp.log1p(jnp.exp(-jnp.abs(z))))


def _gla_inproj_kernel(x_ref, g_ref, w_ref, wg2_ref, bg_ref, y_ref, lf_ref, *, n_main):
    h = _rms(x_ref[...], g_ref[...]).astype(BF16)
    for c in range(0, n_main, 1024):
        y_ref[:, c:c + 1024] = jnp.dot(h, w_ref[:, c:c + 1024], preferred_element_type=F32)
    t = jnp.dot(h, w_ref[:, n_main:], preferred_element_type=F32).astype(BF16)
    z = jnp.dot(t, wg2_ref[...], preferred_element_type=F32) + bg_ref[...]
    lf_ref[...] = _log_sigmoid(z) / GLA_GATE_TAU


def gla_inproj(x, g, w_ext_bf16, wg2_bf16, b_g, row_tile=512):
    rows, d = x.shape
    n_ext = w_ext_bf16.shape[1]
    n_main = n_ext - LANES
    dk_tot = wg2_bf16.shape[1]
    tm = _row_tile(rows, row_tile)
    vmem = 2 * tm * d * 4 + d * n_ext * 2 + 2 * tm * (n_main + dk_tot) * 4 + 2 * tm * 1024 * 4 + (4 << 20)
    return pl.pallas_call(
        functools.partial(_gla_inproj_kernel, n_main=n_main),
        grid=(rows // tm,),
        in_specs=[pl.BlockSpec((tm, d), lambda i: (i, 0)),
                  _resident((1, d), lambda i: (0, 0)),
                  _resident((d, n_ext), lambda i: (0, 0)),
                  _resident((LANES, dk_tot), lambda i: (0, 0)),
                  _resident((1, dk_tot), lambda i: (0, 0))],
        out_specs=[pl.BlockSpec((tm, n_main), lambda i: (i, 0)),
                   pl.BlockSpec((tm, dk_tot), lambda i: (i, 0))],
        out_shape=[jax.ShapeDtypeStruct((rows, n_main), F32),
                   jax.ShapeDtypeStruct((rows, dk_tot), F32)],
        compiler_params=_cparams(("parallel",), vmem),
        name="gla_inproj",
    )(x, g.reshape(1, d), w_ext_bf16, wg2_bf16, b_g.reshape(1, dk_tot))


def _gla_kernel(q_ref, k_ref, v_ref, r_ref, lf_ref, gout_ref, *rest, chunk, n_sub, heads, has_s0, aliased):
    rest = list(rest)
    s0_ref = rest.pop(0) if has_s0 else None
    if aliased:
        rest.pop(0)
    o_ref, s_ref = rest
    group = q_ref.shape[0]
    dk = q_ref.shape[2] // heads
    dv = v_ref.shape[2] // heads

    @pl.when(pl.program_id(1) == 0)
    def _():
        if has_s0:
            s_ref[...] = s0_ref[...]
        else:
            s_ref[...] = jnp.zeros(s_ref.shape, F32)

    op = (lambda a: a.astype(BF16)) if chunk >= 16 else (lambda a: a.astype(BF16).astype(F32))
    row = lax.broadcasted_iota(jnp.int32, (chunk, chunk), 0)
    col = lax.broadcasted_iota(jnp.int32, (chunk, chunk), 1)
    tril = row >= col
    ltri = op(jnp.where(tril, 1.0, 0.0).astype(F32))
    eye = (lax.broadcasted_iota(jnp.int32, (dk, dk), 0) == lax.broadcasted_iota(jnp.int32, (dk, dk), 1))
    mm = functools.partial(jnp.dot, preferred_element_type=F32)

    for u in range(n_sub):
        rs = slice(u * chunk, (u + 1) * chunk)
        for bi in range(group):
            g1, g2, g3 = _split3(lf_ref[bi, rs, :])
            gcum = mm(ltri, op(g1)) + (mm(ltri, op(g2)) + mm(ltri, op(g3)))
            for hh in range(heads):
                ks = slice(hh * dk, (hh + 1) * dk)
                vs = slice(hh * dv, (hh + 1) * dv)
                gh = gcum[:, ks]
                qh = q_ref[bi, rs, ks] * (dk ** -0.5)
                kh = k_ref[bi, rs, ks]
                vh = v_ref[bi, rs, vs]
                g_last = gh[chunk - 1:chunk, :]
                q_dec = op(qh * jnp.exp(gh))
                k_dec = op(kh * jnp.exp(-gh))
                k_rem = op(kh * jnp.exp(g_last - gh))
                vb = op(vh)
                s_old = s_ref[bi, hh]
                a = jnp.where(tril, _dot_nt(q_dec, k_dec), 0.0)
                o = mm(q_dec, op(s_old)) + mm(op(a), vb)
                decay = jnp.sum(jnp.where(eye, jnp.broadcast_to(jnp.exp(g_last), (dk, dk)), 0.0),
                                axis=1, keepdims=True)
                s_ref[bi, hh] = decay * s_old + _dot_tn(k_rem, vb)
                on = _rms(o, gout_ref[:, vs])
                rh = r_ref[bi, rs, vs]
                o_ref[bi, rs, vs] = on * (rh * jax.nn.sigmoid(rh))


def gla_core(y, logf, g_out, s0_all, layer, states, n_layers, batch, seq, row_tile=128, group=4):
    rows, n = y.shape
    dk_tot = logf.shape[1]
    dv_tot = (n - 2 * dk_tot) // 2
    assert dv_tot == 2 * dk_tot
    heads = GLA_HEADS
    dk, dv = dk_tot // heads, dv_tot // heads
    chunk = math.gcd(seq, GLA_CHUNK)
    tl = _row_tile(seq, max(row_tile, chunk))
    n_sub = tl // chunk
    per_seq = seq // tl
    grp = _row_tile(batch, group)
    has_s0 = s0_all is not None
    aliased = states is not None
    y = y.reshape(batch, seq, n)
    logf = logf.reshape(batch, seq, dk_tot)
    state_spec = pl.BlockSpec((None, grp, heads, dk, dv), lambda b, c: (layer, b, 0, 0, 0))
    in_specs = [pl.BlockSpec((grp, tl, dk_tot), lambda b, c: (b, c, 0)),
                pl.BlockSpec((grp, tl, dk_tot), lambda b, c: (b, c, 1)),
                pl.BlockSpec((grp, tl, dv_tot), lambda b, c: (b, c, 1)),
                pl.BlockSpec((grp, tl, dv_tot), lambda b, c: (b, c, 2)),
                pl.BlockSpec((grp, tl, dk_tot), lambda b, c: (b, c, 0)),
                _resident((1, dv_tot), lambda b, c: (0, 0))]
    args = [y, y, y, y, logf, g_out.reshape(1, dv_tot)]
    if has_s0:
        in_specs.append(state_spec)
        args.append(s0_all)
    aliases = {}
    if aliased:
        aliases = {len(args): 1}
        in_specs.append(pl.BlockSpec(memory_space=pl.ANY))
        args.append(states)
    state_bytes = grp * heads * dk * dv * 4
    vmem = grp * (2 * tl * (3 * dk_tot + 2 * dv_tot) * 4 + 2 * tl * dv_tot * 4) + 4 * state_bytes + (16 << 20)
    og, new_states = pl.pallas_call(
        functools.partial(_gla_kernel, chunk=chunk, n_sub=n_sub, heads=heads, has_s0=has_s0, aliased=aliased),
        grid=(batch // grp, per_seq),
        in_specs=in_specs,
        out_specs=[pl.BlockSpec((grp, tl, dv_tot), lambda b, c: (b, c, 0)), state_spec],
        out_shape=[jax.ShapeDtypeStruct((batch, seq, dv_tot), F32),
                   jax.ShapeDtypeStruct((n_layers, batch, heads, dk, dv), F32)],
        input_output_aliases=aliases,
        compiler_params=_cparams(("parallel", "arbitrary"), vmem),
        name="gla_core",
    )(*args)
    return og.reshape(rows, dv_tot), new_states


def _alibi_slopes(n):
    return np.asarray(2.0 ** (-8.0 * np.arange(1, n + 1) / n), dtype=np.float32)


def _moba_qkv_kernel(x_ref, g_ref, w_ref, *rest, heads, dense_kv, aliased):
    rest = list(rest)
    if aliased:
        del rest[:2]
    q_ref = rest.pop(0)
    if dense_kv:
        k_ref, v_ref = rest.pop(0), rest.pop(0)
    kn_ref, vn_ref = rest
    tm = x_ref.shape[0]
    width = q_ref.shape[1]
    hd = width // heads
    h = _rms(x_ref[...], g_ref[...]).astype(BF16)
    q_ref[...] = jnp.dot(h, w_ref[:, :width], preferred_element_type=F32)
    k = jnp.dot(h, w_ref[:, width:2 * width], preferred_element_type=F32)
    v = jnp.dot(h, w_ref[:, 2 * width:], preferred_element_type=F32)
    if dense_kv:
        k_ref[...] = k
        v_ref[...] = v
    for hh in range(heads):
        kn_ref[pl.ds(hh, tm, stride=heads), :] = k[:, hh * hd:(hh + 1) * hd]
        vn_ref[pl.ds(hh, tm, stride=heads), :] = v[:, hh * hd:(hh + 1) * hd]


def moba_qkv(x, g, w_bf16, layer, kv_rows, n_layers, dense_kv, row_tile=512):
    rows, d = x.shape
    width = w_bf16.shape[1] // 3
    heads = MOBA_HEADS
    hd = width // heads
    tm = _row_tile(rows, row_tile)
    aliased = kv_rows is not None
    n_dense = 3 if dense_kv else 1
    in_specs = [pl.BlockSpec((tm, d), lambda i: (i, 0)),
                _resident((1, d), lambda i: (0, 0)),
                _resident((d, 3 * width), lambda i: (0, 0))]
    args = [x, g.reshape(1, d), w_bf16]
    aliases = {}
    if aliased:
        aliases = {3: n_dense, 4: n_dense + 1}
        in_specs += [pl.BlockSpec(memory_space=pl.ANY)] * 2
        args += list(kv_rows)
    dense_spec = pl.BlockSpec((tm, width), lambda i: (i, 0))
    rows_spec = pl.BlockSpec((None, tm * heads, hd), lambda i: (layer, i, 0))
    vmem = 2 * tm * d * 4 + d * 3 * width * 2 + (2 * n_dense + 4 + 3) * tm * width * 4 + (4 << 20)
    return pl.pallas_call(
        functools.partial(_moba_qkv_kernel, heads=heads, dense_kv=dense_kv, aliased=aliased),
        grid=(rows // tm,),
        in_specs=in_specs,
        out_specs=[dense_spec] * n_dense + [rows_spec] * 2,
        out_shape=[jax.ShapeDtypeStruct((rows, width), F32)] * n_dense
                  + [jax.ShapeDtypeStruct((n_layers, rows * heads, hd), F32)] * 2,
        input_output_aliases=aliases,
        compiler_params=_cparams(("parallel",), vmem),
        name="moba_qkv",
    )(*args)


def _topk_mask(gate, block_ids, valid, topk, n_blocks, axis):
    gm = jnp.where(valid, gate, NEG_INF)
    rank = jnp.zeros(gate.shape, F32)
    for m in range(n_blocks):
        gsel = gm[m:m + 1, :] if axis == 0 else gm[:, m:m + 1]
        ahead = jnp.where(gsel > gm, 1.0, jnp.where(gsel == gm, jnp.where(block_ids > m, 1.0, 0.0), 0.0))
        rank = rank + ahead
    return jnp.where(valid, jnp.where(rank < topk, 1.0, 0.0), 0.0)


def _topk_mask_lanes(gate, valid, topk, n_blocks, stride):
    gm = jnp.where(valid, gate, NEG_INF)
    rank = jnp.zeros(gate.shape, F32)
    for r in range(1, n_blocks):
        lower = pltpu.roll(gm, r * stride, axis=1)
        upper = pltpu.roll(gm, LANES - r * stride, axis=1)
        rank = rank + jnp.where(lower >= gm, 1.0, 0.0) + jnp.where(upper > gm, 1.0, 0.0)
    return jnp.where(valid, jnp.where(rank < topk, 1.0, 0.0), 0.0)


def _moba_prompt_kernel(slopes_ref, q_ref, k_ref, v_ref, o_ref, kb_ref, vt_ref, s_ref, bias_ref,
                        *, n_blocks, topk):
    blk = MOBA_BLOCK
    hd = q_ref.shape[1]
    scale = hd ** -0.5
    slope = slopes_ref[pl.program_id(1)]
    key_idx = lax.broadcasted_iota(jnp.int32, (blk, blk), 0)
    qry_idx = lax.broadcasted_iota(jnp.int32, (blk, blk), 1)
    causal = key_idx <= qry_idx
    bias_ref[...] = -slope * (qry_idx - key_idx).astype(F32)
    mm = functools.partial(jnp.dot, preferred_element_type=F32)

    nbp = -(-n_blocks // 8) * 8
    k_means = []
    for n in range(n_blocks):
        kblk = k_ref[n * blk:(n + 1) * blk, :]
        kb_ref[n] = kblk.astype(BF16)
        k_means.append(jnp.mean(kblk, axis=0, keepdims=True))
        vt_ref[n] = v_ref[n * blk:(n + 1) * blk, :].T.astype(BF16)
    if nbp > n_blocks:
        k_means.append(jnp.zeros((nbp - n_blocks, hd), F32))
    km = jnp.concatenate(k_means, axis=0)
    block_ids = lax.broadcasted_iota(jnp.int32, (nbp, blk), 0)

    for i in range(n_blocks):
        q_t = q_ref[i * blk:(i + 1) * blk, :].T
        q_tb = q_t.astype(BF16)
        buf = s_ref.at[i % 2]
        if i > 0:
            gate = _dot_f32(km, q_t)
            sel = _topk_mask(gate, block_ids, block_ids < i, topk, i, axis=0)
        chosen, dist, top = [], [], None
        for j in range(i + 1):
            s = mm(kb_ref[j], q_tb) * scale + bias_ref[...]
            if j == i:
                s = jnp.where(causal, s, NEG_INF)
            buf[j] = s
            mb = jnp.max(s, axis=0, keepdims=True)
            if j < i:
                chosen.append(sel[j:j + 1, :] > 0.5)
                dist.append(slope * float((i - j) * blk))
                mb = jnp.where(chosen[j], mb - dist[j], NEG_INF)
            top = mb if top is None else jnp.maximum(top, mb)
        l = acc = None
        for j in range(i + 1):
            shift = jnp.where(chosen[j], top + dist[j], -NEG_INF) if j < i else top
            p = jnp.exp(buf[j] - shift)
            pl_sum = jnp.sum(p, axis=0, keepdims=True)
            pv = mm(vt_ref[j], p.astype(BF16))
            l = pl_sum if l is None else l + pl_sum
            acc = pv if acc is None else acc + pv
        o_ref[i * blk:(i + 1) * blk, :] = (acc / l).T


def moba_prompt(q, k, v, batch, seq):
    rows, width = q.shape
    heads = MOBA_HEADS
    hd = width // heads
    blk = MOBA_BLOCK
    assert seq % blk == 0 and hd == LANES
    n_blocks = seq // blk
    topk = min(MOBA_TOPK, n_blocks - 1)
    slopes = jnp.asarray(_alibi_slopes(heads))
    strip = pl.BlockSpec((seq, hd), lambda b, h, sl: (b, h))
    grid_spec = pltpu.PrefetchScalarGridSpec(
        num_scalar_prefetch=1,
        grid=(batch, heads),
        in_specs=[strip, strip, strip],
        out_specs=strip,
        scratch_shapes=[pltpu.VMEM((n_blocks, blk, hd), BF16),
                        pltpu.VMEM((n_blocks, hd, blk), BF16),
                        pltpu.VMEM((2, n_blocks, blk, blk), F32),
                        pltpu.VMEM((blk, blk), F32)])
    vmem = 8 * seq * hd * 4 + 2 * seq * hd * 2 + (2 * n_blocks + 12) * blk * blk * 4 + (8 << 20)
    return pl.pallas_call(
        functools.partial(_moba_prompt_kernel, n_blocks=n_blocks, topk=topk),
        grid_spec=grid_spec,
        out_shape=jax.ShapeDtypeStruct((rows, width), F32),
        compiler_params=_cparams(("parallel", "parallel"), vmem),
        name="moba_prompt",
    )(slopes, q, k, v)


def _moba_sample_kernel(pt_ref, q_ref, kn_ref, vn_ref, slope_ref, *rest, n_blocks, ppb, topk, heads, past_len):
    n_pages = n_blocks * ppb
    k_refs = rest[:n_pages]
    v_refs = rest[n_pages:2 * n_pages]
    o_ref = rest[2 * n_pages]
    t_new, width = q_ref.shape
    hd = width // heads
    rows = heads * t_new
    page = k_refs[0].shape[0] // heads
    scale = hd ** -0.5
    row_iota = lax.broadcasted_iota(jnp.int32, (rows, 1), 0)
    row_head = row_iota // t_new
    row_t = row_iota % t_new
    mm = functools.partial(jnp.dot, preferred_element_type=F32)
    head_rows = lambda a, hh: a[hh * t_new:(hh + 1) * t_new, :]
    head_keys = lambda ref, hh: ref[pl.ds(hh, page, stride=heads), :]

    qa = _heads_to_rows(q_ref[...], heads)
    qab = qa.astype(BF16)
    q_heads = [head_rows(qa, hh).astype(BF16) for hh in range(heads)]
    slope = slope_ref[:, 0:1]
    q_pos = (past_len + row_t).astype(F32)
    lane_key = lax.broadcasted_iota(jnp.int32, (rows, page), 1)

    scores, k_means = [], []
    for b in range(n_blocks):
        km = jnp.zeros((heads, hd), F32)
        for pg in range(b * ppb, (b + 1) * ppb):
            k_heads = [head_keys(k_refs[pg], hh) for hh in range(heads)]
            km = km + jnp.concatenate([kh.mean(axis=0, keepdims=True) for kh in k_heads], axis=0)
            scores.append(jnp.concatenate(
                [_dot_nt(q_heads[hh], k_heads[hh].astype(BF16)) for hh in range(heads)], axis=0))
        k_means.append(km / ppb)
    k_means.append(jnp.zeros((LANES - n_blocks * heads, hd), F32))
    lane_g = lax.broadcasted_iota(jnp.int32, (rows, LANES), 1)
    gate = _dot_f32(qa, jnp.concatenate(k_means, axis=0), nt=True)
    valid = ((lane_g % heads) == row_head) & (lane_g < n_blocks * heads)
    sel = _topk_mask_lanes(gate, valid, topk, n_blocks, heads)
    chosen = [jnp.sum(sel[:, b * heads:(b + 1) * heads], axis=1, keepdims=True) > 0.5
              for b in range(n_blocks)]

    pad = jnp.zeros((LANES - t_new * heads, hd), F32)
    k_own = jnp.concatenate([kn_ref[...], pad], axis=0).astype(BF16)
    v_own = jnp.concatenate([vn_ref[...], pad], axis=0).astype(BF16)
    d_own = (row_t - lane_g // heads).astype(F32)
    s_own = _dot_nt(qab, k_own) * scale - slope * d_own
    s_own = jnp.where(((lane_g % heads) == row_head) & (d_own >= 0), s_own, NEG_INF)
    top = jnp.max(s_own, axis=1, keepdims=True)

    logits = []
    for pg in range(n_pages):
        key_pos = (pg * page + lane_key).astype(F32)
        lg = jnp.where(chosen[pg // ppb], scores[pg] * scale - slope * (q_pos - key_pos), NEG_INF)
        top = jnp.maximum(top, jnp.max(lg, axis=1, keepdims=True))
        logits.append(lg)
    p_own = jnp.exp(s_own - top)
    l = jnp.sum(p_own, axis=1, keepdims=True)
    probs = []
    for pg in range(n_pages):
        p = jnp.exp(logits[pg] - top)
        l = l + jnp.sum(p, axis=1, keepdims=True)
        probs.append(p)
    outs = []
    for hh in range(heads):
        acc = None
        for pg in range(n_pages):
            pv = mm(head_rows(probs[pg], hh).astype(BF16), head_keys(v_refs[pg], hh).astype(BF16))
            acc = pv if acc is None else acc + pv
        outs.append(acc)
    acc = jnp.concatenate(outs, axis=0) + mm(p_own.astype(BF16), v_own)
    o_ref[...] = _rows_to_heads(acc / l, heads)


def moba_sample(q, kv_rows, cache_k, cache_v, layer, page_table):
    bs, t_new, width = q.shape
    heads = MOBA_HEADS
    hd = width // heads
    page = cache_k.shape[2]
    n_pages = page_table.shape[1]
    past_len = n_pages * page
    ppb = MOBA_BLOCK // page
    n_blocks = past_len // MOBA_BLOCK
    assert n_blocks * ppb == n_pages and n_blocks >= 1
    assert t_new * heads <= LANES and 2 * n_blocks * heads <= LANES and heads % 8 == 0
    topk = min(MOBA_TOPK, n_blocks)
    rows = heads * t_new
    slope_rows = jnp.asarray(np.repeat(_alibi_slopes(heads), t_new)[:, None] * np.ones((1, LANES), np.float32))
    cache_k = cache_k.reshape(cache_k.shape[0], cache_k.shape[1], page * heads, hd)
    cache_v = cache_v.reshape(cache_v.shape[0], cache_v.shape[1], page * heads, hd)

    def paged(pg):
        return pl.BlockSpec((None, None, page * heads, hd), lambda b, pt: (layer, pt[b, pg], 0, 0))

    new_spec = pl.BlockSpec((None, rows, hd), lambda b, pt: (layer, b, 0))
    seq_spec = pl.BlockSpec((None, t_new, width), lambda b, pt: (b, 0, 0))
    grid_spec = pltpu.PrefetchScalarGridSpec(
        num_scalar_prefetch=1,
        grid=(bs,),
        in_specs=[seq_spec, new_spec, new_spec, _resident((rows, LANES), lambda b, pt: (0, 0))]
                 + [paged(pg) for pg in range(n_pages)] * 2,
        out_specs=seq_spec)
    vmem = 4 * n_pages * page * width * 4 + 6 * n_pages * rows * page * 4 + (8 << 20)
    return pl.pallas_call(
        functools.partial(_moba_sample_kernel, n_blocks=n_blocks, ppb=ppb, topk=topk, heads=heads,
                          past_len=past_len),
        grid_spec=grid_spec,
        out_shape=jax.ShapeDtypeStruct((bs, t_new, width), F32),
        compiler_params=_cparams(("parallel",), vmem),
        name="moba_sample",
    )(page_table, q, kv_rows[0], kv_rows[1], slope_rows, *([cache_k] * n_pages), *([cache_v] * n_pages))


def kernel(x_prompt, x_sample, cache_moba_k, cache_moba_v, state_gla, cache_mem_k, cache_mem_v, page_table, mem_prompt, g_mix, g_cross, g_mlp, g_final, w_gla_in, w_gla_g1, w_gla_g2, b_gla_g, g_gla_out, w_gla_o, w_moba_qkv, w_moba_o, g_mem, w_mem_kv, w_xq, w_xo, w_up, w_down):
    bp, sp, d = x_prompt.shape
    bs, ts, _ = x_sample.shape
    depth = g_mix.shape[0]
    xp = x_prompt.reshape(bp * sp, d)
    xs = x_sample.reshape(bs * ts, d)
    bf = lambda w: w.astype(BF16)

    n_gla = state_gla.shape[0]
    n_moba, _, _, m_heads, m_hd = cache_moba_k.shape
    m_width = m_heads * m_hd
    mem_len = mem_prompt.shape[1]
    mem_rows = mem_prompt.reshape(bp * mem_len, d)
    x_hd = cache_mem_k.shape[4]
    xw = w_xq.shape[2]
    mem_k_rows = cache_mem_k.reshape(depth, bs, mem_len * X_HEADS, x_hd)
    mem_v_rows = cache_mem_v.reshape(depth, bs, mem_len * X_HEADS, x_hd)
    rank = w_gla_g1.shape[2]

    gla_p = gla_s = kv_p = kv_s = None
    memk, memv = [], []
    for i in range(depth):
        j = i // N_MIXERS
        if i % N_MIXERS == 0:
            w_ext = bf(jnp.concatenate([w_gla_in[j], jnp.pad(w_gla_g1[j], ((0, 0), (0, LANES - rank)))], axis=1))
            wg2 = bf(jnp.pad(w_gla_g2[j], ((0, LANES - rank), (0, 0))))
            wo = bf(w_gla_o[j])
            yp, lfp = gla_inproj(xp, g_mix[i], w_ext, wg2, b_gla_g[j])
            ogp, gla_p = gla_core(yp, lfp, g_gla_out[j], None, j, gla_p, n_gla, bp, sp)
            xp = matmul_residual(xp, ogp, wo)
            ys, lfs = gla_inproj(xs, g_mix[i], w_ext, wg2, b_gla_g[j])
            ogs, gla_s = gla_core(ys, lfs, g_gla_out[j], state_gla, j, gla_s, n_gla, bs, ts)
            xs = matmul_residual(xs, ogs, wo)
        else:
            wqkv = bf(w_moba_qkv[j])
            wo = bf(w_moba_o[j])
            qp, kp, vp, *kv_p = moba_qkv(xp, g_mix[i], wqkv, j, kv_p, n_moba, dense_kv=True)
            ap = moba_prompt(qp, kp, vp, bp, sp)
            xp = matmul_residual(xp, ap, wo)
            qs, *kv_s = moba_qkv(xs, g_mix[i], wqkv, j, kv_s, n_moba, dense_kv=False)
            a_s = moba_sample(qs.reshape(bs, ts, m_width), kv_s, cache_moba_k, cache_moba_v, j, page_table)
            xs = matmul_residual(xs, a_s.reshape(bs * ts, m_width), wo)
        (memkv,) = norm_matmul(mem_rows, g_mem[i], bf(w_mem_kv[i]), (2 * xw,))
        memkv = memkv.reshape(bp, mem_len, 2 * xw)
        memk.append(memkv[:, :, :xw].reshape(bp, mem_len, X_HEADS, x_hd))
        memv.append(memkv[:, :, xw:].reshape(bp, mem_len, X_HEADS, x_hd))
        wq = bf(w_xq[i])
        wxo = bf(w_xo[i])
        xp = xattn_prompt(xp, g_cross[i], wq, memkv, wxo, sp)
        (qx,) = norm_matmul(xs, g_cross[i], wq, (xw,))
        ax = xattn_sample(qx.reshape(bs, ts, xw), mem_k_rows, mem_v_rows, i)
        xs = matmul_residual(xs, ax.reshape(bs * ts, xw), wxo)
        gf = g_final if i == depth - 1 else None
        wu, wd = bf(w_up[i]), bf(w_down[i])
        xp = mlp(xp, g_mlp[i], wu, wd, gf)
        xs = mlp(xs, g_mlp[i], wu, wd, gf)
    return (xp.reshape(bp, sp, d), xs.reshape(bs, ts, d),
            kv_p[0].reshape(n_moba, bp, sp, m_heads, m_hd), kv_p[1].reshape(n_moba, bp, sp, m_heads, m_hd),
            kv_s[0].reshape(n_moba, bs, ts, m_heads, m_hd), kv_s[1].reshape(n_moba, bs, ts, m_heads, m_hd),
            gla_p, gla_s, jnp.stack(memk), jnp.stack(memv))
```

```python
import functools
import math

import numpy as np
import jax
import jax.numpy as jnp
from jax import lax
from jax.experimental import pallas as pl
from jax.experimental.pallas import tpu as pltpu

F32 = jnp.float32
BF16 = jnp.bfloat16

EPS = 1e-6
NEG_INF = -1e30
N_MIXERS = 2
GLA_HEADS = 4
GLA_GATE_TAU = 16.0
GLA_CHUNK = 64
MOBA_HEADS = 8
MOBA_BLOCK = 256
MOBA_TOPK = 3
X_HEADS = 4

LANES = 128
V7X_VMEM_BYTES = 64 << 20
VMEM_CAP_BYTES = V7X_VMEM_BYTES - (8 << 20)


def _cparams(sem, vmem_bytes):
    return pltpu.CompilerParams(dimension_semantics=sem,
                                vmem_limit_bytes=int(min(max(vmem_bytes, 16 << 20), VMEM_CAP_BYTES)))


def _row_tile(rows, pref):
    t = min(rows, pref)
    while rows % t:
        t //= 2
    return t


def _resident(shape, index_map):
    return pl.BlockSpec(shape, index_map, pipeline_mode=pl.Buffered(1))


def _layer_weight(w):
    stacked, layer = w
    _, k, n = stacked.shape
    return stacked, pl.BlockSpec((None, k, n), lambda *_: (layer, 0, 0), pipeline_mode=pl.Buffered(1)), (k, n)


def _rms(x, g):
    return x * lax.rsqrt(jnp.mean(x * x, axis=-1, keepdims=True) + EPS) * g


def _split3(a):
    a1 = a.astype(BF16).astype(F32)
    r = a - a1
    a2 = r.astype(BF16).astype(F32)
    a3 = (r - a2).astype(BF16).astype(F32)
    return a1, a2, a3


def _dot_nt(a, b):
    return lax.dot_general(a, b, (((1,), (1,)), ((), ())), preferred_element_type=F32)


def _dot_tn(a, b):
    return lax.dot_general(a, b, (((0,), (0,)), ((), ())), preferred_element_type=F32)


def _dot_f32(a, b, nt=False):
    a1, a2, a3 = _split3(a)
    b1, b2, b3 = _split3(b)
    mm = _dot_nt if nt else functools.partial(jnp.dot, preferred_element_type=F32)
    cast = lambda t: t.astype(BF16)
    small = mm(cast(a1), cast(b3)) + mm(cast(a2), cast(b2)) + mm(cast(a3), cast(b1))
    mid = mm(cast(a1), cast(b2)) + mm(cast(a2), cast(b1))
    return mm(cast(a1), cast(b1)) + (mid + small)


def _heads_to_rows(x, heads):
    hd = x.shape[1] // heads
    return jnp.concatenate([x[:, hh * hd:(hh + 1) * hd] for hh in range(heads)], axis=0)


def _rows_to_heads(x, heads):
    t = x.shape[0] // heads
    return jnp.concatenate([x[hh * t:(hh + 1) * t, :] for hh in range(heads)], axis=1)


def _norm_matmul_kernel(x_ref, g_ref, w_ref, *out_refs, splits):
    h = _rms(x_ref[...], g_ref[...]).astype(BF16)
    off = 0
    for o_ref, n in zip(out_refs, splits):
        o_ref[...] = jnp.dot(h, w_ref[:, off:off + n], preferred_element_type=F32)
        off += n


def norm_matmul(x, g, w, splits, row_tile=512):
    rows, d = x.shape
    w_bf16, w_spec, (_, n) = _layer_weight(w)
    assert sum(splits) == n
    tm = _row_tile(rows, row_tile)
    vmem = 2 * tm * d * 4 + d * n * 2 + 2 * tm * n * 4 + 2 * tm * max(splits) * 4 + (4 << 20)
    return pl.pallas_call(
        functools.partial(_norm_matmul_kernel, splits=tuple(splits)),
        grid=(rows // tm,),
        in_specs=[pl.BlockSpec((tm, d), lambda i: (i, 0)),
                  _resident((1, d), lambda i: (0, 0)),
                  w_spec],
        out_specs=[pl.BlockSpec((tm, s), lambda i: (i, 0)) for s in splits],
        out_shape=[jax.ShapeDtypeStruct((rows, s), F32) for s in splits],
        compiler_params=_cparams(("parallel",), vmem),
        name="norm_matmul",
    )(x, g.reshape(1, d), w_bf16)


def _matmul_residual_kernel(x_ref, a_ref, w_ref, o_ref):
    o_ref[...] = x_ref[...] + jnp.dot(a_ref[...].astype(BF16), w_ref[...], preferred_element_type=F32)


def matmul_residual(x, a, w, row_tile=512):
    rows, d = x.shape
    w_bf16, w_spec, (k, _) = _layer_weight(w)
    tm = _row_tile(rows, row_tile)
    vmem = 4 * tm * d * 4 + 2 * tm * k * 4 + k * d * 2 + tm * d * 4 + (4 << 20)
    return pl.pallas_call(
        _matmul_residual_kernel,
        grid=(rows // tm,),
        in_specs=[pl.BlockSpec((tm, d), lambda i: (i, 0)),
                  pl.BlockSpec((tm, k), lambda i: (i, 0)),
                  w_spec],
        out_specs=pl.BlockSpec((tm, d), lambda i: (i, 0)),
        out_shape=jax.ShapeDtypeStruct((rows, d), F32),
        compiler_params=_cparams(("parallel",), vmem),
        name="matmul_residual",
    )(x, a, w_bf16)


def _softmax_rows(s):
    e = jnp.exp(s - jnp.max(s, axis=-1, keepdims=True))
    return e / jnp.sum(e, axis=-1, keepdims=True)


def _xattn_rows(x, g, wq_ref, kv, wo_ref, heads):
    h = _rms(x, g).astype(BF16)
    q = jnp.dot(h, wq_ref[...], preferred_element_type=F32)
    xw = wq_ref.shape[1]
    hd = xw // heads
    outs = []
    for hh in range(heads):
        qh = q[:, hh * hd:(hh + 1) * hd].astype(BF16)
        kh = kv[:, hh * hd:(hh + 1) * hd].astype(BF16)
        vh = kv[:, xw + hh * hd:xw + (hh + 1) * hd].astype(BF16)
        p = _softmax_rows(_dot_nt(qh, kh) * (hd ** -0.5))
        outs.append(jnp.dot(p.astype(BF16), vh, preferred_element_type=F32).astype(BF16))
    return jnp.dot(jnp.concatenate(outs, axis=1), wo_ref[...], preferred_element_type=F32)


def _mlp_rows(x, g, wu_ref, wd_ref, ff_tile):
    h = _rms(x, g).astype(BF16)
    acc = x
    for c in range(wu_ref.shape[1] // ff_tile):
        a = jnp.dot(h, wu_ref[:, c * ff_tile:(c + 1) * ff_tile], preferred_element_type=F32)
        a = jnp.maximum(a, 0.0)
        a = (a * a).astype(BF16)
        acc = acc + jnp.dot(a, wd_ref[c * ff_tile:(c + 1) * ff_tile, :], preferred_element_type=F32)
    return acc


def _post_mixer_kernel(*refs, heads, ff_tile, has_xattn, final_norm):
    refs = list(refs)
    x_ref, a_ref, wa_ref = refs[:3]
    del refs[:3]
    x = x_ref[...] + jnp.dot(a_ref[...].astype(BF16), wa_ref[...], preferred_element_type=F32)
    if has_xattn:
        gx_ref, wq_ref, kv_ref, wxo_ref = refs[:4]
        del refs[:4]
        x = x + _xattn_rows(x, gx_ref[...], wq_ref, kv_ref[...], wxo_ref, heads)
    gm_ref, wu_ref, wd_ref = refs[:3]
    del refs[:3]
    x = _mlp_rows(x, gm_ref[...], wu_ref, wd_ref, ff_tile)
    if final_norm:
        x = _rms(x, refs.pop(0)[...])
    (o_ref,) = refs
    o_ref[...] = x


def post_mixer(x, a, wa, xattn, g_mlp, wu, wd, g_final=None, row_tile=512, ff_tile=1024):
    rows, d = x.shape
    wa_arr, wa_spec, (ka, _) = _layer_weight(wa)
    wu_arr, wu_spec, (_, d_ff) = _layer_weight(wu)
    wd_arr, wd_spec, _ = _layer_weight(wd)
    has_xattn = xattn is not None
    final_norm = g_final is not None
    tm = _row_tile(xattn[4] if has_xattn else rows, row_tile)
    const = lambda i: (0, 0)
    in_specs = [pl.BlockSpec((tm, d), lambda i: (i, 0)),
                pl.BlockSpec((tm, ka), lambda i: (i, 0)),
                wa_spec]
    args = [x, a, wa_arr]
    vmem = 4 * tm * d * 4 + 2 * tm * ka * 4 + ka * d * 2 + 2 * d * d_ff * 2 + tm * ff_tile * 6 + 4 * tm * d * 4 + (4 << 20)
    if has_xattn:
        g_cross, wq, memkv, wxo, seq = xattn
        wq_arr, wq_spec, (_, xw) = _layer_weight(wq)
        wxo_arr, wxo_spec, _ = _layer_weight(wxo)
        mem = memkv.shape[1]
        per_seq = seq // tm
        in_specs += [_resident((1, d), const), wq_spec,
                     pl.BlockSpec((None, mem, 2 * xw), lambda i: (i // per_seq, 0, 0)),
                     wxo_spec]
        args += [g_cross.reshape(1, d), wq_arr, memkv, wxo_arr]
        vmem += 2 * d * xw * 2 + 2 * mem * 2 * xw * 4 + 6 * tm * xw * 4
    in_specs += [_resident((1, d), const), wu_spec, wd_spec]
    args += [g_mlp.reshape(1, d), wu_arr, wd_arr]
    if final_norm:
        in_specs.append(_resident((1, d), const))
        args.append(g_final.reshape(1, d))
    return pl.pallas_call(
        functools.partial(_post_mixer_kernel, heads=X_HEADS, ff_tile=ff_tile, has_xattn=has_xattn,
                          final_norm=final_norm),
        grid=(rows // tm,),
        in_specs=in_specs,
        out_specs=pl.BlockSpec((tm, d), lambda i: (i, 0)),
        out_shape=jax.ShapeDtypeStruct((rows, d), F32),
        compiler_params=_cparams(("parallel",), vmem),
        name="post_mixer",
    )(*args)


def _xattn_sample_kernel(q_ref, k_ref, v_ref, o_ref, *, heads):
    group, t, xw = q_ref.shape
    hd = xw // heads
    rows = heads * t
    n_kv = k_ref.shape[1]
    row_head = lax.broadcasted_iota(jnp.int32, (rows, 1), 0) // t
    own_head = (lax.broadcasted_iota(jnp.int32, (rows, n_kv), 1) % heads) == row_head
    for gi in range(group):
        qa = _heads_to_rows(q_ref[gi], heads).astype(BF16)
        s = _dot_nt(qa, k_ref[gi].astype(BF16)) * (hd ** -0.5)
        p = _softmax_rows(jnp.where(own_head, s, NEG_INF))
        o = jnp.dot(p.astype(BF16), v_ref[gi].astype(BF16), preferred_element_type=F32)
        o_ref[gi] = _rows_to_heads(o, heads)


def xattn_sample(q, mem_k, mem_v, layer, group=8):
    bs, t, xw = q.shape
    n_kv, hd = mem_k.shape[2], mem_k.shape[3]
    g = _row_tile(bs, group)
    vmem = 4 * g * n_kv * hd * 4 + 2 * n_kv * hd * 2 + 6 * X_HEADS * t * n_kv * 4 + 4 * g * t * xw * 4 + (4 << 20)
    kv_spec = pl.BlockSpec((None, g, n_kv, hd), lambda i: (layer, i, 0, 0))
    return pl.pallas_call(
        functools.partial(_xattn_sample_kernel, heads=X_HEADS),
        grid=(bs // g,),
        in_specs=[pl.BlockSpec((g, t, xw), lambda i: (i, 0, 0)), kv_spec, kv_spec],
        out_specs=pl.BlockSpec((g, t, xw), lambda i: (i, 0, 0)),
        out_shape=jax.ShapeDtypeStruct((bs, t, xw), F32),
        compiler_params=_cparams(("parallel",), vmem),
        name="xattn_sample",
    )(q, mem_k, mem_v)


def _log_sigmoid(z):
    return -(jnp.maximum(-z, 0.0) + jnp.log1p(jnp.exp(-jnp.abs(z))))


def _gla_inproj_kernel(x_ref, g_ref, w_ref, wg2_ref, bg_ref, y_ref, lf_ref, *, n_main):
    h = _rms(x_ref[...], g_ref[...]).astype(BF16)
    for c in range(0, n_main, 1024):
        y_ref[:, c:c + 1024] = jnp.dot(h, w_ref[:, c:c + 1024], preferred_element_type=F32)
    t = jnp.dot(h, w_ref[:, n_main:], preferred_element_type=F32).astype(BF16)
    z = jnp.dot(t, wg2_ref[...], preferred_element_type=F32) + bg_ref[...]
    lf_ref[...] = _log_sigmoid(z) / GLA_GATE_TAU


def gla_inproj(x, g, w_ext_bf16, wg2_bf16, b_g, row_tile=512):
    rows, d = x.shape
    n_ext = w_ext_bf16.shape[1]
    n_main = n_ext - LANES
    dk_tot = wg2_bf16.shape[1]
    tm = _row_tile(rows, row_tile)
    vmem = 2 * tm * d * 4 + d * n_ext * 2 + 2 * tm * (n_main + dk_tot) * 4 + 2 * tm * 1024 * 4 + (4 << 20)
    return pl.pallas_call(
        functools.partial(_gla_inproj_kernel, n_main=n_main),
        grid=(rows // tm,),
        in_specs=[pl.BlockSpec((tm, d), lambda i: (i, 0)),
                  _resident((1, d), lambda i: (0, 0)),
                  _resident((d, n_ext), lambda i: (0, 0)),
                  _resident((LANES, dk_tot), lambda i: (0, 0)),
                  _resident((1, dk_tot), lambda i: (0, 0))],
        out_specs=[pl.BlockSpec((tm, n_main), lambda i: (i, 0)),
                   pl.BlockSpec((tm, dk_tot), lambda i: (i, 0))],
        out_shape=[jax.ShapeDtypeStruct((rows, n_main), F32),
                   jax.ShapeDtypeStruct((rows, dk_tot), F32)],
        compiler_params=_cparams(("parallel",), vmem),
        name="gla_inproj",
    )(x, g.reshape(1, d), w_ext_bf16, wg2_bf16, b_g.reshape(1, dk_tot))


def _gla_kernel(q_ref, k_ref, v_ref, r_ref, lf_ref, gout_ref, *rest, chunk, n_sub, heads, has_s0, aliased):
    rest = list(rest)
    s0_ref = rest.pop(0) if has_s0 else None
    if aliased:
        rest.pop(0)
    o_ref, s_ref = rest
    group = q_ref.shape[0]
    dk = q_ref.shape[2] // heads
    dv = v_ref.shape[2] // heads

    @pl.when(pl.program_id(1) == 0)
    def _():
        if has_s0:
            s_ref[...] = s0_ref[...]
        else:
            s_ref[...] = jnp.zeros(s_ref.shape, F32)

    op = (lambda a: a.astype(BF16)) if chunk >= 16 else (lambda a: a.astype(BF16).astype(F32))
    row = lax.broadcasted_iota(jnp.int32, (chunk, chunk), 0)
    col = lax.broadcasted_iota(jnp.int32, (chunk, chunk), 1)
    tril = row >= col
    ltri = op(jnp.where(tril, 1.0, 0.0).astype(F32))
    eye = (lax.broadcasted_iota(jnp.int32, (dk, dk), 0) == lax.broadcasted_iota(jnp.int32, (dk, dk), 1))
    mm = functools.partial(jnp.dot, preferred_element_type=F32)

    for u in range(n_sub):
        rs = slice(u * chunk, (u + 1) * chunk)
        for bi in range(group):
            g1, g2, g3 = _split3(lf_ref[bi, rs, :])
            gcum = mm(ltri, op(g1)) + (mm(ltri, op(g2)) + mm(ltri, op(g3)))
            for hh in range(heads):
                ks = slice(hh * dk, (hh + 1) * dk)
                vs = slice(hh * dv, (hh + 1) * dv)
                gh = gcum[:, ks]
                qh = q_ref[bi, rs, ks] * (dk ** -0.5)
                kh = k_ref[bi, rs, ks]
                vh = v_ref[bi, rs, vs]
                g_last = gh[chunk - 1:chunk, :]
                q_dec = op(qh * jnp.exp(gh))
                k_dec = op(kh * jnp.exp(-gh))
                k_rem = op(kh * jnp.exp(g_last - gh))
                vb = op(vh)
                s_old = s_ref[bi, hh]
                a = jnp.where(tril, _dot_nt(q_dec, k_dec), 0.0)
                o = mm(q_dec, op(s_old)) + mm(op(a), vb)
                decay = jnp.sum(jnp.where(eye, jnp.broadcast_to(jnp.exp(g_last), (dk, dk)), 0.0),
                                axis=1, keepdims=True)
                s_ref[bi, hh] = decay * s_old + _dot_tn(k_rem, vb)
                on = _rms(o, gout_ref[:, vs])
                rh = r_ref[bi, rs, vs]
                o_ref[bi, rs, vs] = on * (rh * jax.nn.sigmoid(rh))


def gla_core(y, logf, g_out, s0_all, layer, states, n_layers, batch, seq, row_tile=128, group=4):
    rows, n = y.shape
    dk_tot = logf.shape[1]
    dv_tot = (n - 2 * dk_tot) // 2
    assert dv_tot == 2 * dk_tot
    heads = GLA_HEADS
    dk, dv = dk_tot // heads, dv_tot // heads
    chunk = math.gcd(seq, GLA_CHUNK)
    tl = _row_tile(seq, max(row_tile, chunk))
    n_sub = tl // chunk
    per_seq = seq // tl
    grp = _row_tile(batch, group)
    has_s0 = s0_all is not None
    aliased = states is not None
    y = y.reshape(batch, seq, n)
    logf = logf.reshape(batch, seq, dk_tot)
    state_spec = pl.BlockSpec((None, grp, heads, dk, dv), lambda b, c: (layer, b, 0, 0, 0))
    in_specs = [pl.BlockSpec((grp, tl, dk_tot), lambda b, c: (b, c, 0)),
                pl.BlockSpec((grp, tl, dk_tot), lambda b, c: (b, c, 1)),
                pl.BlockSpec((grp, tl, dv_tot), lambda b, c: (b, c, 1)),
                pl.BlockSpec((grp, tl, dv_tot), lambda b, c: (b, c, 2)),
                pl.BlockSpec((grp, tl, dk_tot), lambda b, c: (b, c, 0)),
                _resident((1, dv_tot), lambda b, c: (0, 0))]
    args = [y, y, y, y, logf, g_out.reshape(1, dv_tot)]
    if has_s0:
        in_specs.append(state_spec)
        args.append(s0_all)
    aliases = {}
    if aliased:
        aliases = {len(args): 1}
        in_specs.append(pl.BlockSpec(memory_space=pl.ANY))
        args.append(states)
    state_bytes = grp * heads * dk * dv * 4
    vmem = grp * (2 * tl * (3 * dk_tot + 2 * dv_tot) * 4 + 2 * tl * dv_tot * 4) + 4 * state_bytes + (16 << 20)
    og, new_states = pl.pallas_call(
        functools.partial(_gla_kernel, chunk=chunk, n_sub=n_sub, heads=heads, has_s0=has_s0, aliased=aliased),
        grid=(batch // grp, per_seq),
        in_specs=in_specs,
        out_specs=[pl.BlockSpec((grp, tl, dv_tot), lambda b, c: (b, c, 0)), state_spec],
        out_shape=[jax.ShapeDtypeStruct((batch, seq, dv_tot), F32),
                   jax.ShapeDtypeStruct((n_layers, batch, heads, dk, dv), F32)],
        input_output_aliases=aliases,
        compiler_params=_cparams(("parallel", "arbitrary"), vmem),
        name="gla_core",
    )(*args)
    return og.reshape(rows, dv_tot), new_states


def _alibi_slopes(n):
    return np.asarray(2.0 ** (-8.0 * np.arange(1, n + 1) / n), dtype=np.float32)


def _moba_qkv_kernel(x_ref, g_ref, w_ref, *rest, heads, dense_kv, aliased):
    rest = list(rest)
    if aliased:
        del rest[:2]
    q_ref = rest.pop(0)
    if dense_kv:
        k_ref, v_ref = rest.pop(0), rest.pop(0)
    kn_ref, vn_ref = rest
    tm = x_ref.shape[0]
    width = q_ref.shape[1]
    hd = width // heads
    h = _rms(x_ref[...], g_ref[...]).astype(BF16)
    q_ref[...] = jnp.dot(h, w_ref[:, :width], preferred_element_type=F32)
    k = jnp.dot(h, w_ref[:, width:2 * width], preferred_element_type=F32)
    v = jnp.dot(h, w_ref[:, 2 * width:], preferred_element_type=F32)
    if dense_kv:
        k_ref[...] = k
        v_ref[...] = v
    for hh in range(heads):
        kn_ref[pl.ds(hh, tm, stride=heads), :] = k[:, hh * hd:(hh + 1) * hd]
        vn_ref[pl.ds(hh, tm, stride=heads), :] = v[:, hh * hd:(hh + 1) * hd]


def moba_qkv(x, g, w, kv_rows, dense_kv, row_tile=512):
    rows, d = x.shape
    w_bf16, w_spec, (_, n) = _layer_weight(w)
    n_layers, layer = w[0].shape[0], w[1]
    width = n // 3
    heads = MOBA_HEADS
    hd = width // heads
    tm = _row_tile(rows, row_tile)
    aliased = kv_rows is not None
    n_dense = 3 if dense_kv else 1
    in_specs = [pl.BlockSpec((tm, d), lambda i: (i, 0)),
                _resident((1, d), lambda i: (0, 0)),
                w_spec]
    args = [x, g.reshape(1, d), w_bf16]
    aliases = {}
    if aliased:
        aliases = {3: n_dense, 4: n_dense + 1}
        in_specs += [pl.BlockSpec(memory_space=pl.ANY)] * 2
        args += list(kv_rows)
    dense_spec = pl.BlockSpec((tm, width), lambda i: (i, 0))
    rows_spec = pl.BlockSpec((None, tm * heads, hd), lambda i: (layer, i, 0))
    vmem = 2 * tm * d * 4 + d * 3 * width * 2 + (2 * n_dense + 4 + 3) * tm * width * 4 + (4 << 20)
    return pl.pallas_call(
        functools.partial(_moba_qkv_kernel, heads=heads, dense_kv=dense_kv, aliased=aliased),
        grid=(rows // tm,),
        in_specs=in_specs,
        out_specs=[dense_spec] * n_dense + [rows_spec] * 2,
        out_shape=[jax.ShapeDtypeStruct((rows, width), F32)] * n_dense
                  + [jax.ShapeDtypeStruct((n_layers, rows * heads, hd), F32)] * 2,
        input_output_aliases=aliases,
        compiler_params=_cparams(("parallel",), vmem),
        name="moba_qkv",
    )(*args)


def _topk_mask(gate, block_ids, valid, topk, n_blocks, axis):
    gm = jnp.where(valid, gate, NEG_INF)
    rank = jnp.zeros(gate.shape, F32)
    for m in range(n_blocks):
        gsel = gm[m:m + 1, :] if axis == 0 else gm[:, m:m + 1]
        ahead = jnp.where(gsel > gm, 1.0, jnp.where(gsel == gm, jnp.where(block_ids > m, 1.0, 0.0), 0.0))
        rank = rank + ahead
    return jnp.where(valid, jnp.where(rank < topk, 1.0, 0.0), 0.0)


def _topk_mask_lanes(gate, valid, topk, n_blocks, stride):
    gm = jnp.where(valid, gate, NEG_INF)
    rank = jnp.zeros(gate.shape, F32)
    for r in range(1, n_blocks):
        lower = pltpu.roll(gm, r * stride, axis=1)
        upper = pltpu.roll(gm, LANES - r * stride, axis=1)
        rank = rank + jnp.where(lower >= gm, 1.0, 0.0) + jnp.where(upper > gm, 1.0, 0.0)
    return jnp.where(valid, jnp.where(rank < topk, 1.0, 0.0), 0.0)


def _moba_prompt_kernel(slopes_ref, q_ref, k_ref, v_ref, o_ref, kb_ref, vt_ref, s_ref, bias_ref,
                        *, n_blocks, topk):
    blk = MOBA_BLOCK
    hd = q_ref.shape[1]
    scale = hd ** -0.5
    slope = slopes_ref[pl.program_id(1)]
    key_idx = lax.broadcasted_iota(jnp.int32, (blk, blk), 0)
    qry_idx = lax.broadcasted_iota(jnp.int32, (blk, blk), 1)
    causal = key_idx <= qry_idx
    bias_ref[...] = -slope * (qry_idx - key_idx).astype(F32)
    mm = functools.partial(jnp.dot, preferred_element_type=F32)

    nbp = -(-n_blocks // 8) * 8
    k_means = []
    for n in range(n_blocks):
        kblk = k_ref[n * blk:(n + 1) * blk, :]
        kb_ref[n] = kblk.astype(BF16)
        k_means.append(jnp.mean(kblk, axis=0, keepdims=True))
        vt_ref[n] = v_ref[n * blk:(n + 1) * blk, :].T.astype(BF16)
    if nbp > n_blocks:
        k_means.append(jnp.zeros((nbp - n_blocks, hd), F32))
    km = jnp.concatenate(k_means, axis=0)
    block_ids = lax.broadcasted_iota(jnp.int32, (nbp, blk), 0)

    for i in range(n_blocks):
        q_t = q_ref[i * blk:(i + 1) * blk, :].T
        q_tb = (q_t * scale).astype(BF16)
        buf = s_ref.at[i % 2]
        if i > 0:
            gate = _dot_f32(km, q_t)
            sel = _topk_mask(gate, block_ids, block_ids < i, topk, i, axis=0)
        chosen, dist, top = [], [], None
        for j in range(i + 1):
            s = mm(kb_ref[j], q_tb) + bias_ref[...]
            if j == i:
                s = jnp.where(causal, s, NEG_INF)
            buf[j] = s
            mb = jnp.max(s, axis=0, keepdims=True)
            if j < i:
                chosen.append(sel[j:j + 1, :] > 0.5)
                dist.append(slope * float((i - j) * blk))
                mb = jnp.where(chosen[j], mb - dist[j], NEG_INF)
            top = mb if top is None else jnp.maximum(top, mb)
        l = acc = None
        for j in range(i + 1):
            shift = jnp.where(chosen[j], top + dist[j], -NEG_INF) if j < i else top
            p = jnp.exp(buf[j] - shift)
            pl_sum = jnp.sum(p, axis=0, keepdims=True)
            pv = mm(vt_ref[j], p.astype(BF16))
            l = pl_sum if l is None else l + pl_sum
            acc = pv if acc is None else acc + pv
        o_ref[i * blk:(i + 1) * blk, :] = (acc / l).T


def moba_prompt(q, k, v, batch, seq):
    rows, width = q.shape
    heads = MOBA_HEADS
    hd = width // heads
    blk = MOBA_BLOCK
    assert seq % blk == 0 and hd == LANES
    n_blocks = seq // blk
    topk = min(MOBA_TOPK, n_blocks - 1)
    slopes = jnp.asarray(_alibi_slopes(heads))
    strip = pl.BlockSpec((seq, hd), lambda b, h, sl: (b, h))
    grid_spec = pltpu.PrefetchScalarGridSpec(
        num_scalar_prefetch=1,
        grid=(batch, heads),
        in_specs=[strip, strip, strip],
        out_specs=strip,
        scratch_shapes=[pltpu.VMEM((n_blocks, blk, hd), BF16),
                        pltpu.VMEM((n_blocks, hd, blk), BF16),
                        pltpu.VMEM((2, n_blocks, blk, blk), F32),
                        pltpu.VMEM((blk, blk), F32)])
    vmem = 8 * seq * hd * 4 + 2 * seq * hd * 2 + (2 * n_blocks + 12) * blk * blk * 4 + (8 << 20)
    return pl.pallas_call(
        functools.partial(_moba_prompt_kernel, n_blocks=n_blocks, topk=topk),
        grid_spec=grid_spec,
        out_shape=jax.ShapeDtypeStruct((rows, width), F32),
        compiler_params=_cparams(("parallel", "parallel"), vmem),
        name="moba_prompt",
    )(slopes, q, k, v)


def _moba_sample_kernel(pt_ref, q_ref, kn_ref, vn_ref, slope_ref, *rest, n_blocks, ppb, topk, heads, past_len):
    n_pages = n_blocks * ppb
    k_refs = rest[:n_pages]
    v_refs = rest[n_pages:2 * n_pages]
    o_ref = rest[2 * n_pages]
    t_new, width = q_ref.shape
    hd = width // heads
    rows = heads * t_new
    page = k_refs[0].shape[0] // heads
    scale = hd ** -0.5
    row_iota = lax.broadcasted_iota(jnp.int32, (rows, 1), 0)
    row_head = row_iota // t_new
    row_t = row_iota % t_new
    mm = functools.partial(jnp.dot, preferred_element_type=F32)
    head_rows = lambda a, hh: a[hh * t_new:(hh + 1) * t_new, :]
    head_keys = lambda ref, hh: ref[pl.ds(hh, page, stride=heads), :]

    qa = _heads_to_rows(q_ref[...], heads)
    qab = qa.astype(BF16)
    q_heads = [head_rows(qa, hh).astype(BF16) for hh in range(heads)]
    slope = slope_ref[:, 0:1]
    q_pos = (past_len + row_t).astype(F32)
    lane_key = lax.broadcasted_iota(jnp.int32, (rows, page), 1)

    scores, k_means = [], []
    for b in range(n_blocks):
        km = jnp.zeros((heads, hd), F32)
        for pg in range(b * ppb, (b + 1) * ppb):
            k_heads = [head_keys(k_refs[pg], hh) for hh in range(heads)]
            km = km + jnp.concatenate([kh.mean(axis=0, keepdims=True) for kh in k_heads], axis=0)
            scores.append(jnp.concatenate(
                [_dot_nt(q_heads[hh], k_heads[hh].astype(BF16)) for hh in range(heads)], axis=0))
        k_means.append(km / ppb)
    k_means.append(jnp.zeros((LANES - n_blocks * heads, hd), F32))
    lane_g = lax.broadcasted_iota(jnp.int32, (rows, LANES), 1)
    gate = _dot_f32(qa, jnp.concatenate(k_means, axis=0), nt=True)
    valid = ((lane_g % heads) == row_head) & (lane_g < n_blocks * heads)
    sel = _topk_mask_lanes(gate, valid, topk, n_blocks, heads)
    chosen = [jnp.sum(sel[:, b * heads:(b + 1) * heads], axis=1, keepdims=True) > 0.5
              for b in range(n_blocks)]

    pad = jnp.zeros((LANES - t_new * heads, hd), F32)
    k_own = jnp.concatenate([kn_ref[...], pad], axis=0).astype(BF16)
    v_own = jnp.concatenate([vn_ref[...], pad], axis=0).astype(BF16)
    d_own = (row_t - lane_g // heads).astype(F32)
    s_own = _dot_nt(qab, k_own) * scale - slope * d_own
    s_own = jnp.where(((lane_g % heads) == row_head) & (d_own >= 0), s_own, NEG_INF)
    top = jnp.max(s_own, axis=1, keepdims=True)

    logits = []
    for pg in range(n_pages):
        key_pos = (pg * page + lane_key).astype(F32)
        lg = jnp.where(chosen[pg // ppb], scores[pg] * scale - slope * (q_pos - key_pos), NEG_INF)
        top = jnp.maximum(top, jnp.max(lg, axis=1, keepdims=True))
        logits.append(lg)
    p_own = jnp.exp(s_own - top)
    l = jnp.sum(p_own, axis=1, keepdims=True)
    probs = []
    for pg in range(n_pages):
        p = jnp.exp(logits[pg] - top)
        l = l + jnp.sum(p, axis=1, keepdims=True)
        probs.append(p)
    outs = []
    for hh in range(heads):
        acc = None
        for pg in range(n_pages):
            pv = mm(head_rows(probs[pg], hh).astype(BF16), head_keys(v_refs[pg], hh).astype(BF16))
            acc = pv if acc is None else acc + pv
        outs.append(acc)
    acc = jnp.concatenate(outs, axis=0) + mm(p_own.astype(BF16), v_own)
    o_ref[...] = _rows_to_heads(acc / l, heads)


def moba_sample(q, kv_rows, cache_k, cache_v, layer, page_table):
    bs, t_new, width = q.shape
    heads = MOBA_HEADS
    hd = width // heads
    page = cache_k.shape[2]
    n_pages = page_table.shape[1]
    past_len = n_pages * page
    ppb = MOBA_BLOCK // page
    n_blocks = past_len // MOBA_BLOCK
    assert n_blocks * ppb == n_pages and n_blocks >= 1
    assert t_new * heads <= LANES and 2 * n_blocks * heads <= LANES and heads % 8 == 0
    topk = min(MOBA_TOPK, n_blocks)
    rows = heads * t_new
    slope_rows = jnp.asarray(np.repeat(_alibi_slopes(heads), t_new)[:, None] * np.ones((1, LANES), np.float32))
    cache_k = cache_k.reshape(cache_k.shape[0], cache_k.shape[1], page * heads, hd)
    cache_v = cache_v.reshape(cache_v.shape[0], cache_v.shape[1], page * heads, hd)

    def paged(pg):
        return pl.BlockSpec((None, None, page * heads, hd), lambda b, pt: (layer, pt[b, pg], 0, 0))

    new_spec = pl.BlockSpec((None, rows, hd), lambda b, pt: (layer, b, 0))
    seq_spec = pl.BlockSpec((None, t_new, width), lambda b, pt: (b, 0, 0))
    grid_spec = pltpu.PrefetchScalarGridSpec(
        num_scalar_prefetch=1,
        grid=(bs,),
        in_specs=[seq_spec, new_spec, new_spec, _resident((rows, LANES), lambda b, pt: (0, 0))]
                 + [paged(pg) for pg in range(n_pages)] * 2,
        out_specs=seq_spec)
    vmem = 4 * n_pages * page * width * 4 + 6 * n_pages * rows * page * 4 + (8 << 20)
    return pl.pallas_call(
        functools.partial(_moba_sample_kernel, n_blocks=n_blocks, ppb=ppb, topk=topk, heads=heads,
                          past_len=past_len),
        grid_spec=grid_spec,
        out_shape=jax.ShapeDtypeStruct((bs, t_new, width), F32),
        compiler_params=_cparams(("parallel",), vmem),
        name="moba_sample",
    )(page_table, q, kv_rows[0], kv_rows[1], slope_rows, *([cache_k] * n_pages), *([cache_v] * n_pages))


def kernel(x_prompt, x_sample, cache_moba_k, cache_moba_v, state_gla, cache_mem_k, cache_mem_v, page_table, mem_prompt, g_mix, g_cross, g_mlp, g_final, w_gla_in, w_gla_g1, w_gla_g2, b_gla_g, g_gla_out, w_gla_o, w_moba_qkv, w_moba_o, g_mem, w_mem_kv, w_xq, w_xo, w_up, w_down):
    bp, sp, d = x_prompt.shape
    bs, ts, _ = x_sample.shape
    depth = g_mix.shape[0]
    xp = x_prompt.reshape(bp * sp, d)
    xs = x_sample.reshape(bs * ts, d)
    bf = lambda w: w.astype(BF16)

    n_gla = state_gla.shape[0]
    n_moba, _, _, m_heads, m_hd = cache_moba_k.shape
    m_width = m_heads * m_hd
    mem_len = mem_prompt.shape[1]
    mem_rows = mem_prompt.reshape(bp * mem_len, d)
    x_hd = cache_mem_k.shape[4]
    xw = w_xq.shape[2]
    mem_k_rows = cache_mem_k.reshape(depth, bs, mem_len * X_HEADS, x_hd)
    mem_v_rows = cache_mem_v.reshape(depth, bs, mem_len * X_HEADS, x_hd)
    rank = w_gla_g1.shape[2]
    wo_gla, wqkv, wo_moba, w_memkv = bf(w_gla_o), bf(w_moba_qkv), bf(w_moba_o), bf(w_mem_kv)
    wq, wxo, wu, wd = bf(w_xq), bf(w_xo), bf(w_up), bf(w_down)

    gla_p = gla_s = kv_p = kv_s = None
    memk, memv = [], []
    for i in range(depth):
        j = i // N_MIXERS
        if i % N_MIXERS == 0:
            w_ext = bf(jnp.concatenate([w_gla_in[j], jnp.pad(w_gla_g1[j], ((0, 0), (0, LANES - rank)))], axis=1))
            wg2 = bf(jnp.pad(w_gla_g2[j], ((0, LANES - rank), (0, 0))))
            wo = (wo_gla, j)
            yp, lfp = gla_inproj(xp, g_mix[i], w_ext, wg2, b_gla_g[j])
            mix_p, gla_p = gla_core(yp, lfp, g_gla_out[j], None, j, gla_p, n_gla, bp, sp)
            ys, lfs = gla_inproj(xs, g_mix[i], w_ext, wg2, b_gla_g[j])
            ogs, gla_s = gla_core(ys, lfs, g_gla_out[j], state_gla, j, gla_s, n_gla, bs, ts)
            xs = matmul_residual(xs, ogs, wo)
        else:
            wo = (wo_moba, j)
            qp, kp, vp, *kv_p = moba_qkv(xp, g_mix[i], (wqkv, j), kv_p, dense_kv=True)
            mix_p = moba_prompt(qp, kp, vp, bp, sp)
            qs, *kv_s = moba_qkv(xs, g_mix[i], (wqkv, j), kv_s, dense_kv=False)
            a_s = moba_sample(qs.reshape(bs, ts, m_width), kv_s, cache_moba_k, cache_moba_v, j, page_table)
            xs = matmul_residual(xs, a_s.reshape(bs * ts, m_width), wo)
        (memkv,) = norm_matmul(mem_rows, g_mem[i], (w_memkv, i), (2 * xw,))
        memkv = memkv.reshape(bp, mem_len, 2 * xw)
        memk.append(memkv[:, :, :xw].reshape(bp, mem_len, X_HEADS, x_hd))
        memv.append(memkv[:, :, xw:].reshape(bp, mem_len, X_HEADS, x_hd))
        gf = g_final if i == depth - 1 else None
        xp = post_mixer(xp, mix_p, wo, (g_cross[i], (wq, i), memkv, (wxo, i), sp), g_mlp[i], (wu, i), (wd, i), gf)
        (qx,) = norm_matmul(xs, g_cross[i], (wq, i), (xw,))
        ax = xattn_sample(qx.reshape(bs, ts, xw), mem_k_rows, mem_v_rows, i)
        xs = post_mixer(xs, ax.reshape(bs * ts, xw), (wxo, i), None, g_mlp[i], (wu, i), (wd, i), gf)
    return (xp.reshape(bp, sp, d), xs.reshape(bs, ts, d),
            kv_p[0].reshape(n_moba, bp, sp, m_heads, m_hd), kv_p[1].reshape(n_moba, bp, sp, m_heads, m_hd),
            kv_s[0].reshape(n_moba, bs, ts, m_heads, m_hd), kv_s[1].reshape(n_moba, bs, ts, m_heads, m_hd),
            gla_p, gla_s, jnp.stack(memk), jnp.stack(memv))
```

```python
import functools
import math

import numpy as np
import jax
import jax.numpy as jnp
from jax import lax
from jax.experimental import pallas as pl
from jax.experimental.pallas import tpu as pltpu

F32 = jnp.float32
BF16 = jnp.bfloat16

EPS = 1e-6
NEG_INF = -1e30
N_MIXERS = 2
GLA_HEADS = 4
GLA_GATE_TAU = 16.0
GLA_CHUNK = 64
MOBA_HEADS = 8
MOBA_BLOCK = 256
MOBA_TOPK = 3
X_HEADS = 4

LANES = 128
V7X_VMEM_BYTES = 64 << 20
VMEM_CAP_BYTES = V7X_VMEM_BYTES - (8 << 20)


def _cparams(sem, vmem_bytes):
    return pltpu.CompilerParams(dimension_semantics=sem,
                                vmem_limit_bytes=int(min(max(vmem_bytes, 16 << 20), VMEM_CAP_BYTES)))


def _row_tile(rows, pref):
    t = min(rows, pref)
    while rows % t:
        t //= 2
    return t


def _resident(shape, index_map):
    return pl.BlockSpec(shape, index_map, pipeline_mode=pl.Buffered(1))


def _layer_weight(w):
    stacked, layer = w
    _, k, n = stacked.shape
    return stacked, pl.BlockSpec((None, k, n), lambda *_: (layer, 0, 0), pipeline_mode=pl.Buffered(1)), (k, n)


def _rms(x, g):
    return x * lax.rsqrt(jnp.mean(x * x, axis=-1, keepdims=True) + EPS) * g


def _split3(a):
    a1 = a.astype(BF16).astype(F32)
    r = a - a1
    a2 = r.astype(BF16).astype(F32)
    a3 = (r - a2).astype(BF16).astype(F32)
    return a1, a2, a3


def _dot_nt(a, b):
    return lax.dot_general(a, b, (((1,), (1,)), ((), ())), preferred_element_type=F32)


def _dot_tn(a, b):
    return lax.dot_general(a, b, (((0,), (0,)), ((), ())), preferred_element_type=F32)


def _dot_f32(a, b, nt=False):
    a1, a2, a3 = _split3(a)
    b1, b2, b3 = _split3(b)
    mm = _dot_nt if nt else functools.partial(jnp.dot, preferred_element_type=F32)
    cast = lambda t: t.astype(BF16)
    small = mm(cast(a1), cast(b3)) + mm(cast(a2), cast(b2)) + mm(cast(a3), cast(b1))
    mid = mm(cast(a1), cast(b2)) + mm(cast(a2), cast(b1))
    return mm(cast(a1), cast(b1)) + (mid + small)


def _heads_to_rows(x, heads):
    hd = x.shape[1] // heads
    return jnp.concatenate([x[:, hh * hd:(hh + 1) * hd] for hh in range(heads)], axis=0)


def _rows_to_heads(x, heads):
    t = x.shape[0] // heads
    return jnp.concatenate([x[hh * t:(hh + 1) * t, :] for hh in range(heads)], axis=1)


def _norm_matmul_kernel(x_ref, g_ref, w_ref, *out_refs, splits):
    h = _rms(x_ref[...], g_ref[...]).astype(BF16)
    off = 0
    for o_ref, n in zip(out_refs, splits):
        o_ref[...] = jnp.dot(h, w_ref[:, off:off + n], preferred_element_type=F32)
        off += n


def norm_matmul(x, g, w, splits, row_tile=512):
    rows, d = x.shape
    w_bf16, w_spec, (_, n) = _layer_weight(w)
    assert sum(splits) == n
    tm = _row_tile(rows, row_tile)
    vmem = 2 * tm * d * 4 + d * n * 2 + 2 * tm * n * 4 + 2 * tm * max(splits) * 4 + (4 << 20)
    return pl.pallas_call(
        functools.partial(_norm_matmul_kernel, splits=tuple(splits)),
        grid=(rows // tm,),
        in_specs=[pl.BlockSpec((tm, d), lambda i: (i, 0)),
                  _resident((1, d), lambda i: (0, 0)),
                  w_spec],
        out_specs=[pl.BlockSpec((tm, s), lambda i: (i, 0)) for s in splits],
        out_shape=[jax.ShapeDtypeStruct((rows, s), F32) for s in splits],
        compiler_params=_cparams(("parallel",), vmem),
        name="norm_matmul",
    )(x, g.reshape(1, d), w_bf16)


def _matmul_residual_kernel(x_ref, a_ref, w_ref, o_ref):
    o_ref[...] = x_ref[...] + jnp.dot(a_ref[...].astype(BF16), w_ref[...], preferred_element_type=F32)


def matmul_residual(x, a, w, row_tile=512):
    rows, d = x.shape
    w_bf16, w_spec, (k, _) = _layer_weight(w)
    tm = _row_tile(rows, row_tile)
    vmem = 4 * tm * d * 4 + 2 * tm * k * 4 + k * d * 2 + tm * d * 4 + (4 << 20)
    return pl.pallas_call(
        _matmul_residual_kernel,
        grid=(rows // tm,),
        in_specs=[pl.BlockSpec((tm, d), lambda i: (i, 0)),
                  pl.BlockSpec((tm, k), lambda i: (i, 0)),
                  w_spec],
        out_specs=pl.BlockSpec((tm, d), lambda i: (i, 0)),
        out_shape=jax.ShapeDtypeStruct((rows, d), F32),
        compiler_params=_cparams(("parallel",), vmem),
        name="matmul_residual",
    )(x, a, w_bf16)


def _softmax_rows(s):
    e = jnp.exp(s - jnp.max(s, axis=-1, keepdims=True))
    return e / jnp.sum(e, axis=-1, keepdims=True)


def _xattn_rows(x, g, wq_ref, kv, wo_ref, heads):
    h = _rms(x, g).astype(BF16)
    q = jnp.dot(h, wq_ref[...], preferred_element_type=F32)
    xw = wq_ref.shape[1]
    hd = xw // heads
    outs = []
    for hh in range(heads):
        qh = q[:, hh * hd:(hh + 1) * hd].astype(BF16)
        kh = kv[:, hh * hd:(hh + 1) * hd].astype(BF16)
        vh = kv[:, xw + hh * hd:xw + (hh + 1) * hd].astype(BF16)
        p = _softmax_rows(_dot_nt(qh, kh) * (hd ** -0.5))
        outs.append(jnp.dot(p.astype(BF16), vh, preferred_element_type=F32).astype(BF16))
    return jnp.dot(jnp.concatenate(outs, axis=1), wo_ref[...], preferred_element_type=F32)


def _mlp_rows(x, g, wu_ref, wd_ref, ff_tile):
    h = _rms(x, g).astype(BF16)
    acc = x
    for c in range(wu_ref.shape[1] // ff_tile):
        a = jnp.dot(h, wu_ref[:, c * ff_tile:(c + 1) * ff_tile], preferred_element_type=F32)
        a = jnp.maximum(a, 0.0)
        a = (a * a).astype(BF16)
        acc = acc + jnp.dot(a, wd_ref[c * ff_tile:(c + 1) * ff_tile, :], preferred_element_type=F32)
    return acc


def _post_mixer_kernel(*refs, heads, ff_tile, has_xattn, final_norm):
    refs = list(refs)
    x_ref, a_ref, wa_ref = refs[:3]
    del refs[:3]
    x = x_ref[...] + jnp.dot(a_ref[...].astype(BF16), wa_ref[...], preferred_element_type=F32)
    if has_xattn:
        gx_ref, wq_ref, kv_ref, wxo_ref = refs[:4]
        del refs[:4]
        x = x + _xattn_rows(x, gx_ref[...], wq_ref, kv_ref[...], wxo_ref, heads)
    gm_ref, wu_ref, wd_ref = refs[:3]
    del refs[:3]
    x = _mlp_rows(x, gm_ref[...], wu_ref, wd_ref, ff_tile)
    if final_norm:
        x = _rms(x, refs.pop(0)[...])
    (o_ref,) = refs
    o_ref[...] = x


def post_mixer(x, a, wa, xattn, g_mlp, wu, wd, g_final=None, row_tile=512, ff_tile=1024):
    rows, d = x.shape
    wa_arr, wa_spec, (ka, _) = _layer_weight(wa)
    wu_arr, wu_spec, (_, d_ff) = _layer_weight(wu)
    wd_arr, wd_spec, _ = _layer_weight(wd)
    has_xattn = xattn is not None
    final_norm = g_final is not None
    tm = _row_tile(xattn[4] if has_xattn else rows, row_tile)
    const = lambda i: (0, 0)
    in_specs = [pl.BlockSpec((tm, d), lambda i: (i, 0)),
                pl.BlockSpec((tm, ka), lambda i: (i, 0)),
                wa_spec]
    args = [x, a, wa_arr]
    vmem = 4 * tm * d * 4 + 2 * tm * ka * 4 + ka * d * 2 + 2 * d * d_ff * 2 + tm * ff_tile * 6 + 4 * tm * d * 4 + (4 << 20)
    if has_xattn:
        g_cross, wq, memkv, wxo, seq = xattn
        wq_arr, wq_spec, (_, xw) = _layer_weight(wq)
        wxo_arr, wxo_spec, _ = _layer_weight(wxo)
        mem = memkv.shape[1]
        per_seq = seq // tm
        in_specs += [_resident((1, d), const), wq_spec,
                     pl.BlockSpec((None, mem, 2 * xw), lambda i: (i // per_seq, 0, 0)),
                     wxo_spec]
        args += [g_cross.reshape(1, d), wq_arr, memkv, wxo_arr]
        vmem += 2 * d * xw * 2 + 2 * mem * 2 * xw * 4 + 6 * tm * xw * 4
    in_specs += [_resident((1, d), const), wu_spec, wd_spec]
    args += [g_mlp.reshape(1, d), wu_arr, wd_arr]
    if final_norm:
        in_specs.append(_resident((1, d), const))
        args.append(g_final.reshape(1, d))
    return pl.pallas_call(
        functools.partial(_post_mixer_kernel, heads=X_HEADS, ff_tile=ff_tile, has_xattn=has_xattn,
                          final_norm=final_norm),
        grid=(rows // tm,),
        in_specs=in_specs,
        out_specs=pl.BlockSpec((tm, d), lambda i: (i, 0)),
        out_shape=jax.ShapeDtypeStruct((rows, d), F32),
        compiler_params=_cparams(("parallel",), vmem),
        name="post_mixer",
    )(*args)


def _xattn_sample_kernel(q_ref, k_ref, v_ref, o_ref, *, heads):
    group, t, xw = q_ref.shape
    hd = xw // heads
    rows = heads * t
    n_kv = k_ref.shape[1]
    row_head = lax.broadcasted_iota(jnp.int32, (rows, 1), 0) // t
    own_head = (lax.broadcasted_iota(jnp.int32, (rows, n_kv), 1) % heads) == row_head
    scores = [_dot_nt(_heads_to_rows(q_ref[gi], heads).astype(BF16), k_ref[gi].astype(BF16))
              for gi in range(group)]
    probs = [_softmax_rows(jnp.where(own_head, s * (hd ** -0.5), NEG_INF)).astype(BF16) for s in scores]
    for gi in range(group):
        o = jnp.dot(probs[gi], v_ref[gi].astype(BF16), preferred_element_type=F32)
        o_ref[gi] = _rows_to_heads(o, heads)


def xattn_sample(q, mem_k, mem_v, layer, group=8):
    bs, t, xw = q.shape
    n_kv, hd = mem_k.shape[2], mem_k.shape[3]
    g = _row_tile(bs, group)
    vmem = 4 * g * n_kv * hd * 4 + 2 * n_kv * hd * 2 + 6 * X_HEADS * t * n_kv * 4 + 4 * g * t * xw * 4 + (4 << 20)
    kv_spec = pl.BlockSpec((None, g, n_kv, hd), lambda i: (layer, i, 0, 0))
    return pl.pallas_call(
        functools.partial(_xattn_sample_kernel, heads=X_HEADS),
        grid=(bs // g,),
        in_specs=[pl.BlockSpec((g, t, xw), lambda i: (i, 0, 0)), kv_spec, kv_spec],
        out_specs=pl.BlockSpec((g, t, xw), lambda i: (i, 0, 0)),
        out_shape=jax.ShapeDtypeStruct((bs, t, xw), F32),
        compiler_params=_cparams(("parallel",), vmem),
        name="xattn_sample",
    )(q, mem_k, mem_v)


def _log_sigmoid(z):
    return -(jnp.maximum(-z, 0.0) + jnp.log1p(jnp.exp(-jnp.abs(z))))


def _gla_inproj_kernel(x_ref, g_ref, w_ref, wg2_ref, bg_ref, y_ref, lf_ref, *, n_main):
    h = _rms(x_ref[...], g_ref[...]).astype(BF16)
    for c in range(0, n_main, 1024):
        y_ref[:, c:c + 1024] = jnp.dot(h, w_ref[:, c:c + 1024], preferred_element_type=F32)
    t = jnp.dot(h, w_ref[:, n_main:], preferred_element_type=F32).astype(BF16)
    z = jnp.dot(t, wg2_ref[...], preferred_element_type=F32) + bg_ref[...]
    lf_ref[...] = _log_sigmoid(z) / GLA_GATE_TAU


def gla_inproj(x, g, w_ext_bf16, wg2_bf16, b_g, row_tile=512):
    rows, d = x.shape
    n_ext = w_ext_bf16.shape[1]
    n_main = n_ext - LANES
    dk_tot = wg2_bf16.shape[1]
    tm = _row_tile(rows, row_tile)
    vmem = 2 * tm * d * 4 + d * n_ext * 2 + 2 * tm * (n_main + dk_tot) * 4 + 2 * tm * 1024 * 4 + (4 << 20)
    return pl.pallas_call(
        functools.partial(_gla_inproj_kernel, n_main=n_main),
        grid=(rows // tm,),
        in_specs=[pl.BlockSpec((tm, d), lambda i: (i, 0)),
                  _resident((1, d), lambda i: (0, 0)),
                  _resident((d, n_ext), lambda i: (0, 0)),
                  _resident((LANES, dk_tot), lambda i: (0, 0)),
                  _resident((1, dk_tot), lambda i: (0, 0))],
        out_specs=[pl.BlockSpec((tm, n_main), lambda i: (i, 0)),
                   pl.BlockSpec((tm, dk_tot), lambda i: (i, 0))],
        out_shape=[jax.ShapeDtypeStruct((rows, n_main), F32),
                   jax.ShapeDtypeStruct((rows, dk_tot), F32)],
        compiler_params=_cparams(("parallel",), vmem),
        name="gla_inproj",
    )(x, g.reshape(1, d), w_ext_bf16, wg2_bf16, b_g.reshape(1, dk_tot))


def _gla_kernel(q_ref, k_ref, v_ref, r_ref, lf_ref, gout_ref, *rest, chunk, n_sub, heads, has_s0, aliased):
    rest = list(rest)
    s0_ref = rest.pop(0) if has_s0 else None
    if aliased:
        rest.pop(0)
    o_ref, s_ref = rest
    group = q_ref.shape[0]
    dk = q_ref.shape[2] // heads
    dv = v_ref.shape[2] // heads

    @pl.when(pl.program_id(1) == 0)
    def _():
        if has_s0:
            s_ref[...] = s0_ref[...]
        else:
            s_ref[...] = jnp.zeros(s_ref.shape, F32)

    op = (lambda a: a.astype(BF16)) if chunk >= 16 else (lambda a: a.astype(BF16).astype(F32))
    row = lax.broadcasted_iota(jnp.int32, (chunk, chunk), 0)
    col = lax.broadcasted_iota(jnp.int32, (chunk, chunk), 1)
    tril = row >= col
    ltri = op(jnp.where(tril, 1.0, 0.0).astype(F32))
    eye = (lax.broadcasted_iota(jnp.int32, (dk, dk), 0) == lax.broadcasted_iota(jnp.int32, (dk, dk), 1))
    mm = functools.partial(jnp.dot, preferred_element_type=F32)

    units = [(bi, hh) for bi in range(group) for hh in range(heads)]
    for u in range(n_sub):
        rs = slice(u * chunk, (u + 1) * chunk)
        gcum = []
        for bi in range(group):
            g1, g2, g3 = _split3(lf_ref[bi, rs, :])
            gcum.append(mm(ltri, op(g1)) + (mm(ltri, op(g2)) + mm(ltri, op(g3))))
        q_dec, k_dec, k_rem, vb, g_last = {}, {}, {}, {}, {}
        for bi, hh in units:
            ks = slice(hh * dk, (hh + 1) * dk)
            gh = gcum[bi][:, ks]
            kh = k_ref[bi, rs, ks]
            g_last[bi, hh] = gh[chunk - 1:chunk, :]
            q_dec[bi, hh] = op(q_ref[bi, rs, ks] * (dk ** -0.5) * jnp.exp(gh))
            k_dec[bi, hh] = op(kh * jnp.exp(-gh))
            k_rem[bi, hh] = op(kh * jnp.exp(g_last[bi, hh] - gh))
            vb[bi, hh] = op(v_ref[bi, rs, hh * dv:(hh + 1) * dv])
        a_raw = {un: _dot_nt(q_dec[un], k_dec[un]) for un in units}
        s_add = {un: _dot_tn(k_rem[un], vb[un]) for un in units}
        o_inter = {un: mm(q_dec[un], op(s_ref[un[0], un[1]])) for un in units}
        o_intra = {un: mm(op(jnp.where(tril, a_raw[un], 0.0)), vb[un]) for un in units}
        for bi, hh in units:
            vs = slice(hh * dv, (hh + 1) * dv)
            decay = jnp.sum(jnp.where(eye, jnp.broadcast_to(jnp.exp(g_last[bi, hh]), (dk, dk)), 0.0),
                            axis=1, keepdims=True)
            s_ref[bi, hh] = decay * s_ref[bi, hh] + s_add[bi, hh]
            on = _rms(o_inter[bi, hh] + o_intra[bi, hh], gout_ref[:, vs])
            rh = r_ref[bi, rs, vs]
            o_ref[bi, rs, vs] = on * (rh * jax.nn.sigmoid(rh))


def gla_core(y, logf, g_out, s0_all, layer, states, n_layers, batch, seq, row_tile=128, group=4):
    rows, n = y.shape
    dk_tot = logf.shape[1]
    dv_tot = (n - 2 * dk_tot) // 2
    assert dv_tot == 2 * dk_tot
    heads = GLA_HEADS
    dk, dv = dk_tot // heads, dv_tot // heads
    chunk = math.gcd(seq, GLA_CHUNK)
    tl = _row_tile(seq, max(row_tile, chunk))
    n_sub = tl // chunk
    per_seq = seq // tl
    grp = _row_tile(batch, group)
    has_s0 = s0_all is not None
    aliased = states is not None
    y = y.reshape(batch, seq, n)
    logf = logf.reshape(batch, seq, dk_tot)
    state_spec = pl.BlockSpec((None, grp, heads, dk, dv), lambda b, c: (layer, b, 0, 0, 0))
    in_specs = [pl.BlockSpec((grp, tl, dk_tot), lambda b, c: (b, c, 0)),
                pl.BlockSpec((grp, tl, dk_tot), lambda b, c: (b, c, 1)),
                pl.BlockSpec((grp, tl, dv_tot), lambda b, c: (b, c, 1)),
                pl.BlockSpec((grp, tl, dv_tot), lambda b, c: (b, c, 2)),
                pl.BlockSpec((grp, tl, dk_tot), lambda b, c: (b, c, 0)),
                _resident((1, dv_tot), lambda b, c: (0, 0))]
    args = [y, y, y, y, logf, g_out.reshape(1, dv_tot)]
    if has_s0:
        in_specs.append(state_spec)
        args.append(s0_all)
    aliases = {}
    if aliased:
        aliases = {len(args): 1}
        in_specs.append(pl.BlockSpec(memory_space=pl.ANY))
        args.append(states)
    state_bytes = grp * heads * dk * dv * 4
    vmem = grp * (2 * tl * (3 * dk_tot + 2 * dv_tot) * 4 + 2 * tl * dv_tot * 4) + 4 * state_bytes + (16 << 20)
    og, new_states = pl.pallas_call(
        functools.partial(_gla_kernel, chunk=chunk, n_sub=n_sub, heads=heads, has_s0=has_s0, aliased=aliased),
        grid=(batch // grp, per_seq),
        in_specs=in_specs,
        out_specs=[pl.BlockSpec((grp, tl, dv_tot), lambda b, c: (b, c, 0)), state_spec],
        out_shape=[jax.ShapeDtypeStruct((batch, seq, dv_tot), F32),
                   jax.ShapeDtypeStruct((n_layers, batch, heads, dk, dv), F32)],
        input_output_aliases=aliases,
        compiler_params=_cparams(("parallel", "arbitrary"), vmem),
        name="gla_core",
    )(*args)
    return og.reshape(rows, dv_tot), new_states


def _alibi_slopes(n):
    return np.asarray(2.0 ** (-8.0 * np.arange(1, n + 1) / n), dtype=np.float32)


def _moba_qkv_kernel(x_ref, g_ref, w_ref, *rest, heads, dense_kv, aliased):
    rest = list(rest)
    if aliased:
        del rest[:2]
    q_ref = rest.pop(0)
    if dense_kv:
        k_ref, v_ref = rest.pop(0), rest.pop(0)
    kn_ref, vn_ref = rest
    tm = x_ref.shape[0]
    width = q_ref.shape[1]
    hd = width // heads
    h = _rms(x_ref[...], g_ref[...]).astype(BF16)
    q_ref[...] = jnp.dot(h, w_ref[:, :width], preferred_element_type=F32)
    k = jnp.dot(h, w_ref[:, width:2 * width], preferred_element_type=F32)
    v = jnp.dot(h, w_ref[:, 2 * width:], preferred_element_type=F32)
    if dense_kv:
        k_ref[...] = k
        v_ref[...] = v
    for hh in range(heads):
        kn_ref[pl.ds(hh, tm, stride=heads), :] = k[:, hh * hd:(hh + 1) * hd]
        vn_ref[pl.ds(hh, tm, stride=heads), :] = v[:, hh * hd:(hh + 1) * hd]


def moba_qkv(x, g, w, kv_rows, dense_kv, row_tile=512):
    rows, d = x.shape
    w_bf16, w_spec, (_, n) = _layer_weight(w)
    n_layers, layer = w[0].shape[0], w[1]
    width = n // 3
    heads = MOBA_HEADS
    hd = width // heads
    tm = _row_tile(rows, row_tile)
    aliased = kv_rows is not None
    n_dense = 3 if dense_kv else 1
    in_specs = [pl.BlockSpec((tm, d), lambda i: (i, 0)),
                _resident((1, d), lambda i: (0, 0)),
                w_spec]
    args = [x, g.reshape(1, d), w_bf16]
    aliases = {}
    if aliased:
        aliases = {3: n_dense, 4: n_dense + 1}
        in_specs += [pl.BlockSpec(memory_space=pl.ANY)] * 2
        args += list(kv_rows)
    dense_spec = pl.BlockSpec((tm, width), lambda i: (i, 0))
    rows_spec = pl.BlockSpec((None, tm * heads, hd), lambda i: (layer, i, 0))
    vmem = 2 * tm * d * 4 + d * 3 * width * 2 + (2 * n_dense + 4 + 3) * tm * width * 4 + (4 << 20)
    return pl.pallas_call(
        functools.partial(_moba_qkv_kernel, heads=heads, dense_kv=dense_kv, aliased=aliased),
        grid=(rows // tm,),
        in_specs=in_specs,
        out_specs=[dense_spec] * n_dense + [rows_spec] * 2,
        out_shape=[jax.ShapeDtypeStruct((rows, width), F32)] * n_dense
                  + [jax.ShapeDtypeStruct((n_layers, rows * heads, hd), F32)] * 2,
        input_output_aliases=aliases,
        compiler_params=_cparams(("parallel",), vmem),
        name="moba_qkv",
    )(*args)


def _topk_mask(gate, block_ids, valid, topk, n_blocks, axis):
    gm = jnp.where(valid, gate, NEG_INF)
    rank = jnp.zeros(gate.shape, F32)
    for m in range(n_blocks):
        gsel = gm[m:m + 1, :] if axis == 0 else gm[:, m:m + 1]
        ahead = jnp.where(gsel > gm, 1.0, jnp.where(gsel == gm, jnp.where(block_ids > m, 1.0, 0.0), 0.0))
        rank = rank + ahead
    return jnp.where(valid, jnp.where(rank < topk, 1.0, 0.0), 0.0)


def _topk_mask_lanes(gate, valid, topk, n_blocks, stride):
    gm = jnp.where(valid, gate, NEG_INF)
    rank = jnp.zeros(gate.shape, F32)
    for r in range(1, n_blocks):
        lower = pltpu.roll(gm, r * stride, axis=1)
        upper = pltpu.roll(gm, LANES - r * stride, axis=1)
        rank = rank + jnp.where(lower >= gm, 1.0, 0.0) + jnp.where(upper > gm, 1.0, 0.0)
    return jnp.where(valid, jnp.where(rank < topk, 1.0, 0.0), 0.0)


def _moba_prompt_kernel(slopes_ref, zero_ref, q_ref, k_ref, v_ref, o_ref, kb_ref, vt_ref, s0_ref, s1_ref, bias_ref,
                        *, n_blocks, topk):
    blk = MOBA_BLOCK
    hd = q_ref.shape[1]
    scale = hd ** -0.5
    slope = slopes_ref[pl.program_id(1)]
    key_idx = lax.broadcasted_iota(jnp.int32, (blk, blk), 0)
    qry_idx = lax.broadcasted_iota(jnp.int32, (blk, blk), 1)
    causal = key_idx <= qry_idx
    bias_ref[...] = -slope * (qry_idx - key_idx).astype(F32)
    mm = functools.partial(jnp.dot, preferred_element_type=F32)

    nbp = -(-n_blocks // 8) * 8
    k_means = []
    for n in range(n_blocks):
        kblk = k_ref[n * blk:(n + 1) * blk, :]
        kb_ref[n] = kblk.astype(BF16)
        k_means.append(jnp.mean(kblk, axis=0, keepdims=True))
        vt_ref[n] = v_ref[n * blk:(n + 1) * blk, :].T.astype(BF16)
    if nbp > n_blocks:
        k_means.append(jnp.zeros((nbp - n_blocks, hd), F32))
    km = jnp.concatenate(k_means, axis=0)
    block_ids = lax.broadcasted_iota(jnp.int32, (nbp, blk), 0)

    def begin(i):
        q_t = q_ref[i * blk:(i + 1) * blk, :].T
        st = dict(i=i, buf=(s0_ref, s1_ref)[i % 2], chosen=[], dist=[], top=None, l=None, acc=None,
                  q_tb=(q_t * scale).astype(BF16))
        if i > 0:
            gate = _dot_f32(km, q_t)
            st["sel"] = _topk_mask(gate, block_ids, block_ids < i, topk, i, axis=0)
        return st

    def logits(st, j):
        i = st["i"]
        s = mm(kb_ref[j], st["q_tb"]) + bias_ref[...]
        if j == i:
            s = jnp.where(causal, s, NEG_INF)
        st["buf"][j] = s
        mb = jnp.max(s, axis=0, keepdims=True)
        if j < i:
            st["chosen"].append(st["sel"][j:j + 1, :] > 0.5)
            st["dist"].append(slope * float((i - j) * blk))
            mb = jnp.where(st["chosen"][j], mb - st["dist"][j], NEG_INF)
        st["top"] = mb if st["top"] is None else jnp.maximum(st["top"], mb)

    def weigh(st, j):
        top = st["top"]
        shift = jnp.where(st["chosen"][j], top + st["dist"][j], -NEG_INF) if j < st["i"] else top
        p = jnp.exp(st["buf"][j + zero_ref[0]] - shift)
        pl_sum = jnp.sum(p, axis=0, keepdims=True)
        pv = mm(vt_ref[j], p.astype(BF16))
        st["l"] = pl_sum if st["l"] is None else st["l"] + pl_sum
        st["acc"] = pv if st["acc"] is None else st["acc"] + pv

    cur = begin(0)
    logits(cur, 0)
    for i in range(n_blocks):
        nxt = begin(i + 1) if i + 1 < n_blocks else None
        for j in range(i + 2):
            if nxt is not None:
                logits(nxt, j)
            if j <= i:
                weigh(cur, j)
        o_ref[i * blk:(i + 1) * blk, :] = (cur["acc"] / cur["l"]).T
        cur = nxt


def moba_prompt(q, k, v, batch, seq):
    rows, width = q.shape
    heads = MOBA_HEADS
    hd = width // heads
    blk = MOBA_BLOCK
    assert seq % blk == 0 and hd == LANES
    n_blocks = seq // blk
    topk = min(MOBA_TOPK, n_blocks - 1)
    slopes = jnp.asarray(_alibi_slopes(heads))
    strip = pl.BlockSpec((seq, hd), lambda b, h, sl, zero: (b, h))
    grid_spec = pltpu.PrefetchScalarGridSpec(
        num_scalar_prefetch=2,
        grid=(batch, heads),
        in_specs=[strip, strip, strip],
        out_specs=strip,
        scratch_shapes=[pltpu.VMEM((n_blocks, blk, hd), BF16),
                        pltpu.VMEM((n_blocks, hd, blk), BF16),
                        pltpu.VMEM((n_blocks, blk, blk), F32),
                        pltpu.VMEM((n_blocks, blk, blk), F32),
                        pltpu.VMEM((blk, blk), F32)])
    vmem = 8 * seq * hd * 4 + 2 * seq * hd * 2 + (2 * n_blocks + 12) * blk * blk * 4 + (8 << 20)
    return pl.pallas_call(
        functools.partial(_moba_prompt_kernel, n_blocks=n_blocks, topk=topk),
        grid_spec=grid_spec,
        out_shape=jax.ShapeDtypeStruct((rows, width), F32),
        compiler_params=_cparams(("parallel", "parallel"), vmem),
        name="moba_prompt",
    )(slopes, jnp.zeros((1,), jnp.int32), q, k, v)


def _moba_sample_kernel(pt_ref, q_ref, kn_ref, vn_ref, slope_ref, *rest, n_blocks, ppb, topk, heads, past_len):
    n_pages = n_blocks * ppb
    k_refs = rest[:n_pages]
    v_refs = rest[n_pages:2 * n_pages]
    o_ref = rest[2 * n_pages]
    t_new, width = q_ref.shape
    hd = width // heads
    rows = heads * t_new
    page = k_refs[0].shape[0] // heads
    scale = hd ** -0.5
    row_iota = lax.broadcasted_iota(jnp.int32, (rows, 1), 0)
    row_head = row_iota // t_new
    row_t = row_iota % t_new
    mm = functools.partial(jnp.dot, preferred_element_type=F32)
    head_rows = lambda a, hh: a[hh * t_new:(hh + 1) * t_new, :]
    head_keys = lambda ref, hh: ref[pl.ds(hh, page, stride=heads), :]

    qa = _heads_to_rows(q_ref[...], heads)
    qab = qa.astype(BF16)
    q_heads = [head_rows(qa, hh).astype(BF16) for hh in range(heads)]
    slope = slope_ref[:, 0:1]
    q_pos = (past_len + row_t).astype(F32)
    lane_key = lax.broadcasted_iota(jnp.int32, (rows, page), 1)

    scores, k_means = [], []
    for b in range(n_blocks):
        km = jnp.zeros((heads, hd), F32)
        for pg in range(b * ppb, (b + 1) * ppb):
            k_heads = [head_keys(k_refs[pg], hh) for hh in range(heads)]
            km = km + jnp.concatenate([kh.mean(axis=0, keepdims=True) for kh in k_heads], axis=0)
            scores.append(jnp.concatenate(
                [_dot_nt(q_heads[hh], k_heads[hh].astype(BF16)) for hh in range(heads)], axis=0))
        k_means.append(km / ppb)
    k_means.append(jnp.zeros((LANES - n_blocks * heads, hd), F32))
    lane_g = lax.broadcasted_iota(jnp.int32, (rows, LANES), 1)
    gate = _dot_f32(qa, jnp.concatenate(k_means, axis=0), nt=True)
    valid = ((lane_g % heads) == row_head) & (lane_g < n_blocks * heads)
    sel = _topk_mask_lanes(gate, valid, topk, n_blocks, heads)
    chosen = [jnp.sum(sel[:, b * heads:(b + 1) * heads], axis=1, keepdims=True) > 0.5
              for b in range(n_blocks)]

    pad = jnp.zeros((LANES - t_new * heads, hd), F32)
    k_own = jnp.concatenate([kn_ref[...], pad], axis=0).astype(BF16)
    v_own = jnp.concatenate([vn_ref[...], pad], axis=0).astype(BF16)
    d_own = (row_t - lane_g // heads).astype(F32)
    s_own = _dot_nt(qab, k_own) * scale - slope * d_own
    s_own = jnp.where(((lane_g % heads) == row_head) & (d_own >= 0), s_own, NEG_INF)
    top = jnp.max(s_own, axis=1, keepdims=True)

    logits = []
    for pg in range(n_pages):
        key_pos = (pg * page + lane_key).astype(F32)
        lg = jnp.where(chosen[pg // ppb], scores[pg] * scale - slope * (q_pos - key_pos), NEG_INF)
        top = jnp.maximum(top, jnp.max(lg, axis=1, keepdims=True))
        logits.append(lg)
    p_own = jnp.exp(s_own - top)
    l = jnp.sum(p_own, axis=1, keepdims=True)
    probs = []
    for pg in range(n_pages):
        p = jnp.exp(logits[pg] - top)
        l = l + jnp.sum(p, axis=1, keepdims=True)
        probs.append(p)
    outs = []
    for hh in range(heads):
        acc = None
        for pg in range(n_pages):
            pv = mm(head_rows(probs[pg], hh).astype(BF16), head_keys(v_refs[pg], hh).astype(BF16))
            acc = pv if acc is None else acc + pv
        outs.append(acc)
    acc = jnp.concatenate(outs, axis=0) + mm(p_own.astype(BF16), v_own)
    o_ref[...] = _rows_to_heads(acc / l, heads)


def moba_sample(q, kv_rows, cache_k, cache_v, layer, page_table):
    bs, t_new, width = q.shape
    heads = MOBA_HEADS
    hd = width // heads
    page = cache_k.shape[2]
    n_pages = page_table.shape[1]
    past_len = n_pages * page
    ppb = MOBA_BLOCK // page
    n_blocks = past_len // MOBA_BLOCK
    assert n_blocks * ppb == n_pages and n_blocks >= 1
    assert t_new * heads <= LANES and 2 * n_blocks * heads <= LANES and heads % 8 == 0
    topk = min(MOBA_TOPK, n_blocks)
    rows = heads * t_new
    slope_rows = jnp.asarray(np.repeat(_alibi_slopes(heads), t_new)[:, None] * np.ones((1, LANES), np.float32))
    cache_k = cache_k.reshape(cache_k.shape[0], cache_k.shape[1], page * heads, hd)
    cache_v = cache_v.reshape(cache_v.shape[0], cache_v.shape[1], page * heads, hd)

    def paged(pg):
        return pl.BlockSpec((None, None, page * heads, hd), lambda b, pt: (layer, pt[b, pg], 0, 0))

    new_spec = pl.BlockSpec((None, rows, hd), lambda b, pt: (layer, b, 0))
    seq_spec = pl.BlockSpec((None, t_new, width), lambda b, pt: (b, 0, 0))
    grid_spec = pltpu.PrefetchScalarGridSpec(
        num_scalar_prefetch=1,
        grid=(bs,),
        in_specs=[seq_spec, new_spec, new_spec, _resident((rows, LANES), lambda b, pt: (0, 0))]
                 + [paged(pg) for pg in range(n_pages)] * 2,
        out_specs=seq_spec)
    vmem = 4 * n_pages * page * width * 4 + 6 * n_pages * rows * page * 4 + (8 << 20)
    return pl.pallas_call(
        functools.partial(_moba_sample_kernel, n_blocks=n_blocks, ppb=ppb, topk=topk, heads=heads,
                          past_len=past_len),
        grid_spec=grid_spec,
        out_shape=jax.ShapeDtypeStruct((bs, t_new, width), F32),
        compiler_params=_cparams(("parallel",), vmem),
        name="moba_sample",
    )(page_table, q, kv_rows[0], kv_rows[1], slope_rows, *([cache_k] * n_pages), *([cache_v] * n_pages))


def kernel(x_prompt, x_sample, cache_moba_k, cache_moba_v, state_gla, cache_mem_k, cache_mem_v, page_table, mem_prompt, g_mix, g_cross, g_mlp, g_final, w_gla_in, w_gla_g1, w_gla_g2, b_gla_g, g_gla_out, w_gla_o, w_moba_qkv, w_moba_o, g_mem, w_mem_kv, w_xq, w_xo, w_up, w_down):
    bp, sp, d = x_prompt.shape
    bs, ts, _ = x_sample.shape
    depth = g_mix.shape[0]
    xp = x_prompt.reshape(bp * sp, d)
    xs = x_sample.reshape(bs * ts, d)
    bf = lambda w: w.astype(BF16)

    n_gla = state_gla.shape[0]
    n_moba, _, _, m_heads, m_hd = cache_moba_k.shape
    m_width = m_heads * m_hd
    mem_len = mem_prompt.shape[1]
    mem_rows = mem_prompt.reshape(bp * mem_len, d)
    x_hd = cache_mem_k.shape[4]
    xw = w_xq.shape[2]
    mem_k_rows = cache_mem_k.reshape(depth, bs, mem_len * X_HEADS, x_hd)
    mem_v_rows = cache_mem_v.reshape(depth, bs, mem_len * X_HEADS, x_hd)
    rank = w_gla_g1.shape[2]
    wo_gla, wqkv, wo_moba, w_memkv = bf(w_gla_o), bf(w_moba_qkv), bf(w_moba_o), bf(w_mem_kv)
    wq, wxo, wu, wd = bf(w_xq), bf(w_xo), bf(w_up), bf(w_down)

    gla_p = gla_s = kv_p = kv_s = None
    memk, memv = [], []
    for i in range(depth):
        j = i // N_MIXERS
        if i % N_MIXERS == 0:
            w_ext = bf(jnp.concatenate([w_gla_in[j], jnp.pad(w_gla_g1[j], ((0, 0), (0, LANES - rank)))], axis=1))
            wg2 = bf(jnp.pad(w_gla_g2[j], ((0, LANES - rank), (0, 0))))
            wo = (wo_gla, j)
            yp, lfp = gla_inproj(xp, g_mix[i], w_ext, wg2, b_gla_g[j])
            mix_p, gla_p = gla_core(yp, lfp, g_gla_out[j], None, j, gla_p, n_gla, bp, sp)
            ys, lfs = gla_inproj(xs, g_mix[i], w_ext, wg2, b_gla_g[j])
            ogs, gla_s = gla_core(ys, lfs, g_gla_out[j], state_gla, j, gla_s, n_gla, bs, ts)
            xs = matmul_residual(xs, ogs, wo)
        else:
            wo = (wo_moba, j)
            qp, kp, vp, *kv_p = moba_qkv(xp, g_mix[i], (wqkv, j), kv_p, dense_kv=True)
            mix_p = moba_prompt(qp, kp, vp, bp, sp)
            qs, *kv_s = moba_qkv(xs, g_mix[i], (wqkv, j), kv_s, dense_kv=False)
            a_s = moba_sample(qs.reshape(bs, ts, m_width), kv_s, cache_moba_k, cache_moba_v, j, page_table)
            xs = matmul_residual(xs, a_s.reshape(bs * ts, m_width), wo)
        (memkv,) = norm_matmul(mem_rows, g_mem[i], (w_memkv, i), (2 * xw,))
        memkv = memkv.reshape(bp, mem_len, 2 * xw)
        memk.append(memkv[:, :, :xw].reshape(bp, mem_len, X_HEADS, x_hd))
        memv.append(memkv[:, :, xw:].reshape(bp, mem_len, X_HEADS, x_hd))
        gf = g_final if i == depth - 1 else None
        xp = post_mixer(xp, mix_p, wo, (g_cross[i], (wq, i), memkv, (wxo, i), sp), g_mlp[i], (wu, i), (wd, i), gf)
        (qx,) = norm_matmul(xs, g_cross[i], (wq, i), (xw,))
        ax = xattn_sample(qx.reshape(bs, ts, xw), mem_k_rows, mem_v_rows, i)
        xs = post_mixer(xs, ax.reshape(bs * ts, xw), (wxo, i), None, g_mlp[i], (wu, i), (wd, i), gf)
    return (xp.reshape(bp, sp, d), xs.reshape(bs, ts, d),
            kv_p[0].reshape(n_moba, bp, sp, m_heads, m_hd), kv_p[1].reshape(n_moba, bp, sp, m_heads, m_hd),
            kv_s[0].reshape(n_moba, bs, ts, m_heads, m_hd), kv_s[1].reshape(n_moba, bs, ts, m_heads, m_hd),
            gla_p, gla_s, jnp.stack(memk), jnp.stack(memv))
```

```python
import functools
import math

import numpy as np
import jax
import jax.numpy as jnp
from jax import lax
from jax.experimental import pallas as pl
from jax.experimental.pallas import tpu as pltpu

F32 = jnp.float32
BF16 = jnp.bfloat16

EPS = 1e-6
NEG_INF = -1e30
N_MIXERS = 2
GLA_HEADS = 4
GLA_GATE_TAU = 16.0
GLA_CHUNK = 64
MOBA_HEADS = 8
MOBA_BLOCK = 256
MOBA_TOPK = 3
X_HEADS = 4

LANES = 128
ONES_ROWS = 16
V7X_VMEM_BYTES = 64 << 20
VMEM_CAP_BYTES = V7X_VMEM_BYTES - (8 << 20)


def _cparams(sem, vmem_bytes):
    return pltpu.CompilerParams(dimension_semantics=sem,
                                vmem_limit_bytes=int(min(max(vmem_bytes, 16 << 20), VMEM_CAP_BYTES)))


def _row_tile(rows, pref):
    t = min(rows, pref)
    while rows % t:
        t //= 2
    return t


def _resident(shape, index_map):
    return pl.BlockSpec(shape, index_map, pipeline_mode=pl.Buffered(1))


def _layer_weight(w):
    stacked, layer = w
    _, k, n = stacked.shape
    return stacked, pl.BlockSpec((None, k, n), lambda *_: (layer, 0, 0), pipeline_mode=pl.Buffered(1)), (k, n)


def _rms(x, g):
    return x * lax.rsqrt(jnp.mean(x * x, axis=-1, keepdims=True) + EPS) * g


def _split3(a):
    a1 = a.astype(BF16).astype(F32)
    r = a - a1
    a2 = r.astype(BF16).astype(F32)
    a3 = (r - a2).astype(BF16).astype(F32)
    return a1, a2, a3


def _dot_nt(a, b):
    return lax.dot_general(a, b, (((1,), (1,)), ((), ())), preferred_element_type=F32)


def _dot_tn(a, b):
    return lax.dot_general(a, b, (((0,), (0,)), ((), ())), preferred_element_type=F32)


def _dot_f32(a, b, nt=False):
    a1, a2, a3 = _split3(a)
    b1, b2, b3 = _split3(b)
    mm = _dot_nt if nt else functools.partial(jnp.dot, preferred_element_type=F32)
    cast = lambda t: t.astype(BF16)
    small = mm(cast(a1), cast(b3)) + mm(cast(a2), cast(b2)) + mm(cast(a3), cast(b1))
    mid = mm(cast(a1), cast(b2)) + mm(cast(a2), cast(b1))
    return mm(cast(a1), cast(b1)) + (mid + small)


def _heads_to_rows(x, heads):
    hd = x.shape[1] // heads
    return jnp.concatenate([x[:, hh * hd:(hh + 1) * hd] for hh in range(heads)], axis=0)


def _rows_to_heads(x, heads):
    t = x.shape[0] // heads
    return jnp.concatenate([x[hh * t:(hh + 1) * t, :] for hh in range(heads)], axis=1)


def _norm_matmul_kernel(x_ref, g_ref, w_ref, *out_refs, splits):
    h = _rms(x_ref[...], g_ref[...]).astype(BF16)
    off = 0
    for o_ref, n in zip(out_refs, splits):
        o_ref[...] = jnp.dot(h, w_ref[:, off:off + n], preferred_element_type=F32)
        off += n


def norm_matmul(x, g, w, splits, row_tile=512):
    rows, d = x.shape
    w_bf16, w_spec, (_, n) = _layer_weight(w)
    assert sum(splits) == n
    tm = _row_tile(rows, row_tile)
    vmem = 2 * tm * d * 4 + d * n * 2 + 2 * tm * n * 4 + 2 * tm * max(splits) * 4 + (4 << 20)
    return pl.pallas_call(
        functools.partial(_norm_matmul_kernel, splits=tuple(splits)),
        grid=(rows // tm,),
        in_specs=[pl.BlockSpec((tm, d), lambda i: (i, 0)),
                  _resident((1, d), lambda i: (0, 0)),
                  w_spec],
        out_specs=[pl.BlockSpec((tm, s), lambda i: (i, 0)) for s in splits],
        out_shape=[jax.ShapeDtypeStruct((rows, s), F32) for s in splits],
        compiler_params=_cparams(("parallel",), vmem),
        name="norm_matmul",
    )(x, g.reshape(1, d), w_bf16)


def _matmul_residual_kernel(x_ref, a_ref, w_ref, o_ref):
    o_ref[...] = x_ref[...] + jnp.dot(a_ref[...].astype(BF16), w_ref[...], preferred_element_type=F32)


def matmul_residual(x, a, w, row_tile=512):
    rows, d = x.shape
    w_bf16, w_spec, (k, _) = _layer_weight(w)
    tm = _row_tile(rows, row_tile)
    vmem = 4 * tm * d * 4 + 2 * tm * k * 4 + k * d * 2 + tm * d * 4 + (4 << 20)
    return pl.pallas_call(
        _matmul_residual_kernel,
        grid=(rows // tm,),
        in_specs=[pl.BlockSpec((tm, d), lambda i: (i, 0)),
                  pl.BlockSpec((tm, k), lambda i: (i, 0)),
                  w_spec],
        out_specs=pl.BlockSpec((tm, d), lambda i: (i, 0)),
        out_shape=jax.ShapeDtypeStruct((rows, d), F32),
        compiler_params=_cparams(("parallel",), vmem),
        name="matmul_residual",
    )(x, a, w_bf16)


def _softmax_rows(s):
    e = jnp.exp(s - jnp.max(s, axis=-1, keepdims=True))
    return e / jnp.sum(e, axis=-1, keepdims=True)


def _xattn_rows(x, g, wq_ref, kv, wo_ref, heads):
    h = _rms(x, g).astype(BF16)
    q = jnp.dot(h, wq_ref[...], preferred_element_type=F32)
    xw = wq_ref.shape[1]
    hd = xw // heads
    head = lambda a, off, hh: a[:, off + hh * hd:off + (hh + 1) * hd].astype(BF16)
    scores = [_dot_nt(head(q, 0, hh), head(kv, 0, hh)) for hh in range(heads)]
    probs = [_softmax_rows(s * (hd ** -0.5)).astype(BF16) for s in scores]
    outs = [jnp.dot(probs[hh], head(kv, xw, hh), preferred_element_type=F32).astype(BF16) for hh in range(heads)]
    return jnp.dot(jnp.concatenate(outs, axis=1), wo_ref[...], preferred_element_type=F32)


def _mlp_rows(x, g, wu_ref, wd_ref, ff_tile):
    h = _rms(x, g).astype(BF16)
    acc = x
    for c in range(wu_ref.shape[1] // ff_tile):
        a = jnp.dot(h, wu_ref[:, c * ff_tile:(c + 1) * ff_tile], preferred_element_type=F32)
        a = jnp.maximum(a, 0.0)
        a = (a * a).astype(BF16)
        acc = acc + jnp.dot(a, wd_ref[c * ff_tile:(c + 1) * ff_tile, :], preferred_element_type=F32)
    return acc


def _post_mixer_kernel(*refs, heads, ff_tile, has_xattn, final_norm):
    refs = list(refs)
    x_ref, a_ref, wa_ref = refs[:3]
    del refs[:3]
    x = x_ref[...] + jnp.dot(a_ref[...].astype(BF16), wa_ref[...], preferred_element_type=F32)
    if has_xattn:
        gx_ref, wq_ref, kv_ref, wxo_ref = refs[:4]
        del refs[:4]
        x = x + _xattn_rows(x, gx_ref[...], wq_ref, kv_ref[...], wxo_ref, heads)
    gm_ref, wu_ref, wd_ref = refs[:3]
    del refs[:3]
    x = _mlp_rows(x, gm_ref[...], wu_ref, wd_ref, ff_tile)
    if final_norm:
        x = _rms(x, refs.pop(0)[...])
    (o_ref,) = refs
    o_ref[...] = x


def post_mixer(x, a, wa, xattn, g_mlp, wu, wd, g_final=None, row_tile=512, ff_tile=1024):
    rows, d = x.shape
    wa_arr, wa_spec, (ka, _) = _layer_weight(wa)
    wu_arr, wu_spec, (_, d_ff) = _layer_weight(wu)
    wd_arr, wd_spec, _ = _layer_weight(wd)
    has_xattn = xattn is not None
    final_norm = g_final is not None
    tm = _row_tile(xattn[4] if has_xattn else rows, row_tile)
    const = lambda i: (0, 0)
    in_specs = [pl.BlockSpec((tm, d), lambda i: (i, 0)),
                pl.BlockSpec((tm, ka), lambda i: (i, 0)),
                wa_spec]
    args = [x, a, wa_arr]
    vmem = 4 * tm * d * 4 + 2 * tm * ka * 4 + ka * d * 2 + 2 * d * d_ff * 2 + tm * ff_tile * 6 + 4 * tm * d * 4 + (4 << 20)
    if has_xattn:
        g_cross, wq, memkv, wxo, seq = xattn
        wq_arr, wq_spec, (_, xw) = _layer_weight(wq)
        wxo_arr, wxo_spec, _ = _layer_weight(wxo)
        mem = memkv.shape[1]
        per_seq = seq // tm
        in_specs += [_resident((1, d), const), wq_spec,
                     pl.BlockSpec((None, mem, 2 * xw), lambda i: (i // per_seq, 0, 0)),
                     wxo_spec]
        args += [g_cross.reshape(1, d), wq_arr, memkv, wxo_arr]
        vmem += 2 * d * xw * 2 + 2 * mem * 2 * xw * 4 + 6 * tm * xw * 4
    in_specs += [_resident((1, d), const), wu_spec, wd_spec]
    args += [g_mlp.reshape(1, d), wu_arr, wd_arr]
    if final_norm:
        in_specs.append(_resident((1, d), const))
        args.append(g_final.reshape(1, d))
    return pl.pallas_call(
        functools.partial(_post_mixer_kernel, heads=X_HEADS, ff_tile=ff_tile, has_xattn=has_xattn,
                          final_norm=final_norm),
        grid=(rows // tm,),
        in_specs=in_specs,
        out_specs=pl.BlockSpec((tm, d), lambda i: (i, 0)),
        out_shape=jax.ShapeDtypeStruct((rows, d), F32),
        compiler_params=_cparams(("parallel",), vmem),
        name="post_mixer",
    )(*args)


def _xattn_sample_kernel(q_ref, k_ref, v_ref, o_ref, *, heads):
    group, t, xw = q_ref.shape
    hd = xw // heads
    rows = heads * t
    n_kv = k_ref.shape[1]
    row_head = lax.broadcasted_iota(jnp.int32, (rows, 1), 0) // t
    own_head = (lax.broadcasted_iota(jnp.int32, (rows, n_kv), 1) % heads) == row_head
    scores = [_dot_nt(_heads_to_rows(q_ref[gi], heads).astype(BF16), k_ref[gi].astype(BF16))
              for gi in range(group)]
    probs = [_softmax_rows(jnp.where(own_head, s * (hd ** -0.5), NEG_INF)).astype(BF16) for s in scores]
    for gi in range(group):
        o = jnp.dot(probs[gi], v_ref[gi].astype(BF16), preferred_element_type=F32)
        o_ref[gi] = _rows_to_heads(o, heads)


def xattn_sample(q, mem_k, mem_v, layer, group=8):
    bs, t, xw = q.shape
    n_kv, hd = mem_k.shape[2], mem_k.shape[3]
    g = _row_tile(bs, group)
    vmem = 4 * g * n_kv * hd * 4 + 2 * n_kv * hd * 2 + 6 * X_HEADS * t * n_kv * 4 + 4 * g * t * xw * 4 + (4 << 20)
    kv_spec = pl.BlockSpec((None, g, n_kv, hd), lambda i: (layer, i, 0, 0))
    return pl.pallas_call(
        functools.partial(_xattn_sample_kernel, heads=X_HEADS),
        grid=(bs // g,),
        in_specs=[pl.BlockSpec((g, t, xw), lambda i: (i, 0, 0)), kv_spec, kv_spec],
        out_specs=pl.BlockSpec((g, t, xw), lambda i: (i, 0, 0)),
        out_shape=jax.ShapeDtypeStruct((bs, t, xw), F32),
        compiler_params=_cparams(("parallel",), vmem),
        name="xattn_sample",
    )(q, mem_k, mem_v)


def _log_sigmoid(z):
    return -(jnp.maximum(-z, 0.0) + jnp.log1p(jnp.exp(-jnp.abs(z))))


def _gla_inproj_kernel(x_ref, g_ref, w_ref, wg2_ref, bg_ref, y_ref, lf_ref, *, n_main):
    h = _rms(x_ref[...], g_ref[...]).astype(BF16)
    for c in range(0, n_main, 1024):
        y_ref[:, c:c + 1024] = jnp.dot(h, w_ref[:, c:c + 1024], preferred_element_type=F32)
    t = jnp.dot(h, w_ref[:, n_main:], preferred_element_type=F32).astype(BF16)
    z = jnp.dot(t, wg2_ref[...], preferred_element_type=F32) + bg_ref[...]
    lf_ref[...] = _log_sigmoid(z) / GLA_GATE_TAU


def gla_inproj(x, g, w_ext_bf16, wg2_bf16, b_g, row_tile=512):
    rows, d = x.shape
    n_ext = w_ext_bf16.shape[1]
    n_main = n_ext - LANES
    dk_tot = wg2_bf16.shape[1]
    tm = _row_tile(rows, row_tile)
    vmem = 2 * tm * d * 4 + d * n_ext * 2 + 2 * tm * (n_main + dk_tot) * 4 + 2 * tm * 1024 * 4 + (4 << 20)
    return pl.pallas_call(
        functools.partial(_gla_inproj_kernel, n_main=n_main),
        grid=(rows // tm,),
        in_specs=[pl.BlockSpec((tm, d), lambda i: (i, 0)),
                  _resident((1, d), lambda i: (0, 0)),
                  _resident((d, n_ext), lambda i: (0, 0)),
                  _resident((LANES, dk_tot), lambda i: (0, 0)),
                  _resident((1, dk_tot), lambda i: (0, 0))],
        out_specs=[pl.BlockSpec((tm, n_main), lambda i: (i, 0)),
                   pl.BlockSpec((tm, dk_tot), lambda i: (i, 0))],
        out_shape=[jax.ShapeDtypeStruct((rows, n_main), F32),
                   jax.ShapeDtypeStruct((rows, dk_tot), F32)],
        compiler_params=_cparams(("parallel",), vmem),
        name="gla_inproj",
    )(x, g.reshape(1, d), w_ext_bf16, wg2_bf16, b_g.reshape(1, dk_tot))


def _gla_kernel(q_ref, k_ref, v_ref, r_ref, lf_ref, gout_ref, *rest, chunk, n_sub, heads, has_s0, aliased):
    rest = list(rest)
    s0_ref = rest.pop(0) if has_s0 else None
    if aliased:
        rest.pop(0)
    o_ref, s_ref = rest
    group = q_ref.shape[0]
    dk = q_ref.shape[2] // heads
    dv = v_ref.shape[2] // heads

    @pl.when(pl.program_id(1) == 0)
    def _():
        if has_s0:
            s_ref[...] = s0_ref[...]
        else:
            s_ref[...] = jnp.zeros(s_ref.shape, F32)

    op = (lambda a: a.astype(BF16)) if chunk >= 16 else (lambda a: a.astype(BF16).astype(F32))
    row = lax.broadcasted_iota(jnp.int32, (chunk, chunk), 0)
    col = lax.broadcasted_iota(jnp.int32, (chunk, chunk), 1)
    tril = row >= col
    ltri = op(jnp.where(tril, 1.0, 0.0).astype(F32))
    eye = (lax.broadcasted_iota(jnp.int32, (dk, dk), 0) == lax.broadcasted_iota(jnp.int32, (dk, dk), 1))
    mm = functools.partial(jnp.dot, preferred_element_type=F32)

    units = [(bi, hh) for bi in range(group) for hh in range(heads)]
    for u in range(n_sub):
        rs = slice(u * chunk, (u + 1) * chunk)
        gcum = []
        for bi in range(group):
            g1, g2, g3 = _split3(lf_ref[bi, rs, :])
            gcum.append(mm(ltri, op(g1)) + (mm(ltri, op(g2)) + mm(ltri, op(g3))))
        q_dec, k_dec, k_rem, vb, g_last = {}, {}, {}, {}, {}
        for bi, hh in units:
            ks = slice(hh * dk, (hh + 1) * dk)
            gh = gcum[bi][:, ks]
            kh = k_ref[bi, rs, ks]
            g_last[bi, hh] = gh[chunk - 1:chunk, :]
            q_dec[bi, hh] = op(q_ref[bi, rs, ks] * (dk ** -0.5) * jnp.exp(gh))
            k_dec[bi, hh] = op(kh * jnp.exp(-gh))
            k_rem[bi, hh] = op(kh * jnp.exp(g_last[bi, hh] - gh))
            vb[bi, hh] = op(v_ref[bi, rs, hh * dv:(hh + 1) * dv])
        a_raw = {un: _dot_nt(q_dec[un], k_dec[un]) for un in units}
        s_add = {un: _dot_tn(k_rem[un], vb[un]) for un in units}
        o_inter = {un: mm(q_dec[un], op(s_ref[un[0], un[1]])) for un in units}
        o_intra = {un: mm(op(jnp.where(tril, a_raw[un], 0.0)), vb[un]) for un in units}
        for bi, hh in units:
            vs = slice(hh * dv, (hh + 1) * dv)
            decay = jnp.sum(jnp.where(eye, jnp.broadcast_to(jnp.exp(g_last[bi, hh]), (dk, dk)), 0.0),
                            axis=1, keepdims=True)
            s_ref[bi, hh] = decay * s_ref[bi, hh] + s_add[bi, hh]
            on = _rms(o_inter[bi, hh] + o_intra[bi, hh], gout_ref[:, vs])
            rh = r_ref[bi, rs, vs]
            o_ref[bi, rs, vs] = on * (rh * jax.nn.sigmoid(rh))


def gla_core(y, logf, g_out, s0_all, layer, states, n_layers, batch, seq, row_tile=128, group=4):
    rows, n = y.shape
    dk_tot = logf.shape[1]
    dv_tot = (n - 2 * dk_tot) // 2
    assert dv_tot == 2 * dk_tot
    heads = GLA_HEADS
    dk, dv = dk_tot // heads, dv_tot // heads
    chunk = math.gcd(seq, GLA_CHUNK)
    tl = _row_tile(seq, max(row_tile, chunk))
    n_sub = tl // chunk
    per_seq = seq // tl
    grp = _row_tile(batch, group)
    has_s0 = s0_all is not None
    aliased = states is not None
    y = y.reshape(batch, seq, n)
    logf = logf.reshape(batch, seq, dk_tot)
    state_spec = pl.BlockSpec((None, grp, heads, dk, dv), lambda b, c: (layer, b, 0, 0, 0))
    in_specs = [pl.BlockSpec((grp, tl, dk_tot), lambda b, c: (b, c, 0)),
                pl.BlockSpec((grp, tl, dk_tot), lambda b, c: (b, c, 1)),
                pl.BlockSpec((grp, tl, dv_tot), lambda b, c: (b, c, 1)),
                pl.BlockSpec((grp, tl, dv_tot), lambda b, c: (b, c, 2)),
                pl.BlockSpec((grp, tl, dk_tot), lambda b, c: (b, c, 0)),
                _resident((1, dv_tot), lambda b, c: (0, 0))]
    args = [y, y, y, y, logf, g_out.reshape(1, dv_tot)]
    if has_s0:
        in_specs.append(state_spec)
        args.append(s0_all)
    aliases = {}
    if aliased:
        aliases = {len(args): 1}
        in_specs.append(pl.BlockSpec(memory_space=pl.ANY))
        args.append(states)
    state_bytes = grp * heads * dk * dv * 4
    vmem = grp * (2 * tl * (3 * dk_tot + 2 * dv_tot) * 4 + 2 * tl * dv_tot * 4) + 4 * state_bytes + (16 << 20)
    og, new_states = pl.pallas_call(
        functools.partial(_gla_kernel, chunk=chunk, n_sub=n_sub, heads=heads, has_s0=has_s0, aliased=aliased),
        grid=(batch // grp, per_seq),
        in_specs=in_specs,
        out_specs=[pl.BlockSpec((grp, tl, dv_tot), lambda b, c: (b, c, 0)), state_spec],
        out_shape=[jax.ShapeDtypeStruct((batch, seq, dv_tot), F32),
                   jax.ShapeDtypeStruct((n_layers, batch, heads, dk, dv), F32)],
        input_output_aliases=aliases,
        compiler_params=_cparams(("parallel", "arbitrary"), vmem),
        name="gla_core",
    )(*args)
    return og.reshape(rows, dv_tot), new_states


def _alibi_slopes(n):
    return np.asarray(2.0 ** (-8.0 * np.arange(1, n + 1) / n), dtype=np.float32)


def _moba_qkv_kernel(x_ref, g_ref, w_ref, *rest, heads, dense_kv, aliased):
    rest = list(rest)
    if aliased:
        del rest[:2]
    q_ref = rest.pop(0)
    if dense_kv:
        k_ref, v_ref = rest.pop(0), rest.pop(0)
    kn_ref, vn_ref = rest
    tm = x_ref.shape[0]
    width = q_ref.shape[1]
    hd = width // heads
    h = _rms(x_ref[...], g_ref[...]).astype(BF16)
    q_ref[...] = jnp.dot(h, w_ref[:, :width], preferred_element_type=F32)
    k = jnp.dot(h, w_ref[:, width:2 * width], preferred_element_type=F32)
    v = jnp.dot(h, w_ref[:, 2 * width:], preferred_element_type=F32)
    if dense_kv:
        k_ref[...] = k
        v_ref[...] = v
    for hh in range(heads):
        kn_ref[pl.ds(hh, tm, stride=heads), :] = k[:, hh * hd:(hh + 1) * hd]
        vn_ref[pl.ds(hh, tm, stride=heads), :] = v[:, hh * hd:(hh + 1) * hd]


def moba_qkv(x, g, w, kv_rows, dense_kv, row_tile=512):
    rows, d = x.shape
    w_bf16, w_spec, (_, n) = _layer_weight(w)
    n_layers, layer = w[0].shape[0], w[1]
    width = n // 3
    heads = MOBA_HEADS
    hd = width // heads
    tm = _row_tile(rows, row_tile)
    aliased = kv_rows is not None
    n_dense = 3 if dense_kv else 1
    in_specs = [pl.BlockSpec((tm, d), lambda i: (i, 0)),
                _resident((1, d), lambda i: (0, 0)),
                w_spec]
    args = [x, g.reshape(1, d), w_bf16]
    aliases = {}
    if aliased:
        aliases = {3: n_dense, 4: n_dense + 1}
        in_specs += [pl.BlockSpec(memory_space=pl.ANY)] * 2
        args += list(kv_rows)
    dense_spec = pl.BlockSpec((tm, width), lambda i: (i, 0))
    rows_spec = pl.BlockSpec((None, tm * heads, hd), lambda i: (layer, i, 0))
    vmem = 2 * tm * d * 4 + d * 3 * width * 2 + (2 * n_dense + 4 + 3) * tm * width * 4 + (4 << 20)
    return pl.pallas_call(
        functools.partial(_moba_qkv_kernel, heads=heads, dense_kv=dense_kv, aliased=aliased),
        grid=(rows // tm,),
        in_specs=in_specs,
        out_specs=[dense_spec] * n_dense + [rows_spec] * 2,
        out_shape=[jax.ShapeDtypeStruct((rows, width), F32)] * n_dense
                  + [jax.ShapeDtypeStruct((n_layers, rows * heads, hd), F32)] * 2,
        input_output_aliases=aliases,
        compiler_params=_cparams(("parallel",), vmem),
        name="moba_qkv",
    )(*args)


def _topk_mask(gate, block_ids, valid, topk, n_blocks, axis):
    gm = jnp.where(valid, gate, NEG_INF)
    rank = jnp.zeros(gate.shape, F32)
    for m in range(n_blocks):
        gsel = gm[m:m + 1, :] if axis == 0 else gm[:, m:m + 1]
        ahead = jnp.where(gsel > gm, 1.0, jnp.where(gsel == gm, jnp.where(block_ids > m, 1.0, 0.0), 0.0))
        rank = rank + ahead
    return jnp.where(valid, jnp.where(rank < topk, 1.0, 0.0), 0.0)


def _topk_mask_lanes(gate, valid, topk, n_blocks, stride):
    gm = jnp.where(valid, gate, NEG_INF)
    rank = jnp.zeros(gate.shape, F32)
    for r in range(1, n_blocks):
        lower = pltpu.roll(gm, r * stride, axis=1)
        upper = pltpu.roll(gm, LANES - r * stride, axis=1)
        rank = rank + jnp.where(lower >= gm, 1.0, 0.0) + jnp.where(upper > gm, 1.0, 0.0)
    return jnp.where(valid, jnp.where(rank < topk, 1.0, 0.0), 0.0)


def _moba_prompt_kernel(slopes_ref, zero_ref, q_ref, k_ref, v_ref, o_ref, kb_ref, vt_ref, s0_ref, s1_ref, bias_ref,
                        *, n_blocks, topk):
    blk = MOBA_BLOCK
    hd = q_ref.shape[1]
    scale = hd ** -0.5
    slope = slopes_ref[pl.program_id(1)]
    key_idx = lax.broadcasted_iota(jnp.int32, (blk, blk), 0)
    qry_idx = lax.broadcasted_iota(jnp.int32, (blk, blk), 1)
    causal = key_idx <= qry_idx
    bias_ref[...] = -slope * (qry_idx - key_idx).astype(F32)
    mm = functools.partial(jnp.dot, preferred_element_type=F32)

    nbp = -(-n_blocks // 8) * 8
    k_means = []
    for n in range(n_blocks):
        kblk = k_ref[n * blk:(n + 1) * blk, :]
        kb_ref[n] = kblk.astype(BF16)
        k_means.append(jnp.mean(kblk, axis=0, keepdims=True))
        vt_ref[n, :hd, :] = v_ref[n * blk:(n + 1) * blk, :].T.astype(BF16)
        vt_ref[n, hd:, :] = jnp.ones((ONES_ROWS, blk), BF16)
    if nbp > n_blocks:
        k_means.append(jnp.zeros((nbp - n_blocks, hd), F32))
    km = jnp.concatenate(k_means, axis=0)
    block_ids = lax.broadcasted_iota(jnp.int32, (nbp, blk), 0)

    def begin(i):
        q_t = q_ref[i * blk:(i + 1) * blk, :].T
        st = dict(i=i, buf=(s0_ref, s1_ref)[i % 2], chosen=[], dist=[], top=None, acc=None,
                  q_tb=(q_t * scale).astype(BF16))
        if i > 0:
            gate = _dot_f32(km, q_t)
            st["sel"] = _topk_mask(gate, block_ids, block_ids < i, topk, i, axis=0)
        return st

    def logits(st, j):
        i = st["i"]
        s = mm(kb_ref[j], st["q_tb"]) + bias_ref[...]
        if j == i:
            s = jnp.where(causal, s, NEG_INF)
        st["buf"][j] = s
        mb = jnp.max(s, axis=0, keepdims=True)
        if j < i:
            st["chosen"].append(st["sel"][j:j + 1, :] > 0.5)
            st["dist"].append(slope * float((i - j) * blk))
            mb = jnp.where(st["chosen"][j], mb - st["dist"][j], NEG_INF)
        st["top"] = mb if st["top"] is None else jnp.maximum(st["top"], mb)

    def weigh(st, j):
        top = st["top"]
        shift = jnp.where(st["chosen"][j], top + st["dist"][j], -NEG_INF) if j < st["i"] else top
        p = jnp.exp(st["buf"][j + zero_ref[0]] - shift)
        pv = mm(vt_ref[j], p.astype(BF16))
        st["acc"] = pv if st["acc"] is None else st["acc"] + pv

    cur = begin(0)
    logits(cur, 0)
    for i in range(n_blocks):
        nxt = begin(i + 1) if i + 1 < n_blocks else None
        for j in range(i + 2):
            if nxt is not None:
                logits(nxt, j)
            if j <= i:
                weigh(cur, j)
        acc = cur["acc"]
        o_ref[i * blk:(i + 1) * blk, :] = (acc[:hd, :] / acc[hd:hd + 1, :]).T
        cur = nxt


def moba_prompt(q, k, v, batch, seq):
    rows, width = q.shape
    heads = MOBA_HEADS
    hd = width // heads
    blk = MOBA_BLOCK
    assert seq % blk == 0 and hd == LANES
    n_blocks = seq // blk
    topk = min(MOBA_TOPK, n_blocks - 1)
    slopes = jnp.asarray(_alibi_slopes(heads))
    strip = pl.BlockSpec((seq, hd), lambda b, h, sl, zero: (b, h))
    grid_spec = pltpu.PrefetchScalarGridSpec(
        num_scalar_prefetch=2,
        grid=(batch, heads),
        in_specs=[strip, strip, strip],
        out_specs=strip,
        scratch_shapes=[pltpu.VMEM((n_blocks, blk, hd), BF16),
                        pltpu.VMEM((n_blocks, hd + ONES_ROWS, blk), BF16),
                        pltpu.VMEM((n_blocks, blk, blk), F32),
                        pltpu.VMEM((n_blocks, blk, blk), F32),
                        pltpu.VMEM((blk, blk), F32)])
    vmem = 8 * seq * hd * 4 + 2 * seq * hd * 2 + (2 * n_blocks + 12) * blk * blk * 4 + (8 << 20)
    return pl.pallas_call(
        functools.partial(_moba_prompt_kernel, n_blocks=n_blocks, topk=topk),
        grid_spec=grid_spec,
        out_shape=jax.ShapeDtypeStruct((rows, width), F32),
        compiler_params=_cparams(("parallel", "parallel"), vmem),
        name="moba_prompt",
    )(slopes, jnp.zeros((1,), jnp.int32), q, k, v)


def _moba_sample_kernel(pt_ref, q_ref, kn_ref, vn_ref, slope_ref, *rest, n_blocks, ppb, topk, heads, past_len):
    n_pages = n_blocks * ppb
    k_refs = rest[:n_pages]
    v_refs = rest[n_pages:2 * n_pages]
    o_ref = rest[2 * n_pages]
    t_new, width = q_ref.shape
    hd = width // heads
    rows = heads * t_new
    page = k_refs[0].shape[0] // heads
    scale = hd ** -0.5
    row_iota = lax.broadcasted_iota(jnp.int32, (rows, 1), 0)
    row_head = row_iota // t_new
    row_t = row_iota % t_new
    mm = functools.partial(jnp.dot, preferred_element_type=F32)
    head_rows = lambda a, hh: a[hh * t_new:(hh + 1) * t_new, :]
    head_keys = lambda ref, hh: ref[pl.ds(hh, page, stride=heads), :]

    qa = _heads_to_rows(q_ref[...], heads)
    qab = qa.astype(BF16)
    q_heads = [head_rows(qa, hh).astype(BF16) for hh in range(heads)]
    slope = slope_ref[:, 0:1]
    q_pos = (past_len + row_t).astype(F32)
    lane_key = lax.broadcasted_iota(jnp.int32, (rows, page), 1)

    scores, k_means = [], []
    for b in range(n_blocks):
        km = jnp.zeros((heads, hd), F32)
        for pg in range(b * ppb, (b + 1) * ppb):
            k_heads = [head_keys(k_refs[pg], hh) for hh in range(heads)]
            km = km + jnp.concatenate([kh.mean(axis=0, keepdims=True) for kh in k_heads], axis=0)
            scores.append(jnp.concatenate(
                [_dot_nt(q_heads[hh], k_heads[hh].astype(BF16)) for hh in range(heads)], axis=0))
        k_means.append(km / ppb)
    k_means.append(jnp.zeros((LANES - n_blocks * heads, hd), F32))
    lane_g = lax.broadcasted_iota(jnp.int32, (rows, LANES), 1)
    gate = _dot_f32(qa, jnp.concatenate(k_means, axis=0), nt=True)
    valid = ((lane_g % heads) == row_head) & (lane_g < n_blocks * heads)
    sel = _topk_mask_lanes(gate, valid, topk, n_blocks, heads)
    chosen = [jnp.sum(sel[:, b * heads:(b + 1) * heads], axis=1, keepdims=True) > 0.5
              for b in range(n_blocks)]

    pad = jnp.zeros((LANES - t_new * heads, hd), F32)
    k_own = jnp.concatenate([kn_ref[...], pad], axis=0).astype(BF16)
    v_own = jnp.concatenate([vn_ref[...], pad], axis=0).astype(BF16)
    d_own = (row_t - lane_g // heads).astype(F32)
    s_own = _dot_nt(qab, k_own) * scale - slope * d_own
    s_own = jnp.where(((lane_g % heads) == row_head) & (d_own >= 0), s_own, NEG_INF)
    top = jnp.max(s_own, axis=1, keepdims=True)

    logits = []
    for pg in range(n_pages):
        key_pos = (pg * page + lane_key).astype(F32)
        lg = jnp.where(chosen[pg // ppb], scores[pg] * scale - slope * (q_pos - key_pos), NEG_INF)
        top = jnp.maximum(top, jnp.max(lg, axis=1, keepdims=True))
        logits.append(lg)
    p_own = jnp.exp(s_own - top)
    l = jnp.sum(p_own, axis=1, keepdims=True)
    probs = []
    for pg in range(n_pages):
        p = jnp.exp(logits[pg] - top)
        l = l + jnp.sum(p, axis=1, keepdims=True)
        probs.append(p)
    outs = []
    for hh in range(heads):
        acc = None
        for pg in range(n_pages):
            pv = mm(head_rows(probs[pg], hh).astype(BF16), head_keys(v_refs[pg], hh).astype(BF16))
            acc = pv if acc is None else acc + pv
        outs.append(acc)
    acc = jnp.concatenate(outs, axis=0) + mm(p_own.astype(BF16), v_own)
    o_ref[...] = _rows_to_heads(acc / l, heads)


def moba_sample(q, kv_rows, cache_k, cache_v, layer, page_table):
    bs, t_new, width = q.shape
    heads = MOBA_HEADS
    hd = width // heads
    page = cache_k.shape[2]
    n_pages = page_table.shape[1]
    past_len = n_pages * page
    ppb = MOBA_BLOCK // page
    n_blocks = past_len // MOBA_BLOCK
    assert n_blocks * ppb == n_pages and n_blocks >= 1
    assert t_new * heads <= LANES and 2 * n_blocks * heads <= LANES and heads % 8 == 0
    topk = min(MOBA_TOPK, n_blocks)
    rows = heads * t_new
    slope_rows = jnp.asarray(np.repeat(_alibi_slopes(heads), t_new)[:, None] * np.ones((1, LANES), np.float32))
    cache_k = cache_k.reshape(cache_k.shape[0], cache_k.shape[1], page * heads, hd)
    cache_v = cache_v.reshape(cache_v.shape[0], cache_v.shape[1], page * heads, hd)

    def paged(pg):
        return pl.BlockSpec((None, None, page * heads, hd), lambda b, pt: (layer, pt[b, pg], 0, 0))

    new_spec = pl.BlockSpec((None, rows, hd), lambda b, pt: (layer, b, 0))
    seq_spec = pl.BlockSpec((None, t_new, width), lambda b, pt: (b, 0, 0))
    grid_spec = pltpu.PrefetchScalarGridSpec(
        num_scalar_prefetch=1,
        grid=(bs,),
        in_specs=[seq_spec, new_spec, new_spec, _resident((rows, LANES), lambda b, pt: (0, 0))]
                 + [paged(pg) for pg in range(n_pages)] * 2,
        out_specs=seq_spec)
    vmem = 4 * n_pages * page * width * 4 + 6 * n_pages * rows * page * 4 + (8 << 20)
    return pl.pallas_call(
        functools.partial(_moba_sample_kernel, n_blocks=n_blocks, ppb=ppb, topk=topk, heads=heads,
                          past_len=past_len),
        grid_spec=grid_spec,
        out_shape=jax.ShapeDtypeStruct((bs, t_new, width), F32),
        compiler_params=_cparams(("parallel",), vmem),
        name="moba_sample",
    )(page_table, q, kv_rows[0], kv_rows[1], slope_rows, *([cache_k] * n_pages), *([cache_v] * n_pages))


def kernel(x_prompt, x_sample, cache_moba_k, cache_moba_v, state_gla, cache_mem_k, cache_mem_v, page_table, mem_prompt, g_mix, g_cross, g_mlp, g_final, w_gla_in, w_gla_g1, w_gla_g2, b_gla_g, g_gla_out, w_gla_o, w_moba_qkv, w_moba_o, g_mem, w_mem_kv, w_xq, w_xo, w_up, w_down):
    bp, sp, d = x_prompt.shape
    bs, ts, _ = x_sample.shape
    depth = g_mix.shape[0]
    xp = x_prompt.reshape(bp * sp, d)
    xs = x_sample.reshape(bs * ts, d)
    bf = lambda w: w.astype(BF16)

    n_gla = state_gla.shape[0]
    n_moba, _, _, m_heads, m_hd = cache_moba_k.shape
    m_width = m_heads * m_hd
    mem_len = mem_prompt.shape[1]
    mem_rows = mem_prompt.reshape(bp * mem_len, d)
    x_hd = cache_mem_k.shape[4]
    xw = w_xq.shape[2]
    mem_k_rows = cache_mem_k.reshape(depth, bs, mem_len * X_HEADS, x_hd)
    mem_v_rows = cache_mem_v.reshape(depth, bs, mem_len * X_HEADS, x_hd)
    rank = w_gla_g1.shape[2]
    wo_gla, wqkv, wo_moba, w_memkv = bf(w_gla_o), bf(w_moba_qkv), bf(w_moba_o), bf(w_mem_kv)
    wq, wxo, wu, wd = bf(w_xq), bf(w_xo), bf(w_up), bf(w_down)

    gla_p = gla_s = kv_p = kv_s = None
    memk, memv = [], []
    for i in range(depth):
        j = i // N_MIXERS
        if i % N_MIXERS == 0:
            w_ext = bf(jnp.concatenate([w_gla_in[j], jnp.pad(w_gla_g1[j], ((0, 0), (0, LANES - rank)))], axis=1))
            wg2 = bf(jnp.pad(w_gla_g2[j], ((0, LANES - rank), (0, 0))))
            wo = (wo_gla, j)
            yp, lfp = gla_inproj(xp, g_mix[i], w_ext, wg2, b_gla_g[j])
            mix_p, gla_p = gla_core(yp, lfp, g_gla_out[j], None, j, gla_p, n_gla, bp, sp)
            ys, lfs = gla_inproj(xs, g_mix[i], w_ext, wg2, b_gla_g[j])
            ogs, gla_s = gla_core(ys, lfs, g_gla_out[j], state_gla, j, gla_s, n_gla, bs, ts)
            xs = matmul_residual(xs, ogs, wo)
        else:
            wo = (wo_moba, j)
            qp, kp, vp, *kv_p = moba_qkv(xp, g_mix[i], (wqkv, j), kv_p, dense_kv=True)
            mix_p = moba_prompt(qp, kp, vp, bp, sp)
            qs, *kv_s = moba_qkv(xs, g_mix[i], (wqkv, j), kv_s, dense_kv=False)
            a_s = moba_sample(qs.reshape(bs, ts, m_width), kv_s, cache_moba_k, cache_moba_v, j, page_table)
            xs = matmul_residual(xs, a_s.reshape(bs * ts, m_width), wo)
        (memkv,) = norm_matmul(mem_rows, g_mem[i], (w_memkv, i), (2 * xw,))
        memkv = memkv.reshape(bp, mem_len, 2 * xw)
        memk.append(memkv[:, :, :xw].reshape(bp, mem_len, X_HEADS, x_hd))
        memv.append(memkv[:, :, xw:].reshape(bp, mem_len, X_HEADS, x_hd))
        gf = g_final if i == depth - 1 else None
        xp = post_mixer(xp, mix_p, wo, (g_cross[i], (wq, i), memkv, (wxo, i), sp), g_mlp[i], (wu, i), (wd, i), gf)
        (qx,) = norm_matmul(xs, g_cross[i], (wq, i), (xw,))
        ax = xattn_sample(qx.reshape(bs, ts, xw), mem_k_rows, mem_v_rows, i)
        xs = post_mixer(xs, ax.reshape(bs * ts, xw), (wxo, i), None, g_mlp[i], (wu, i), (wd, i), gf)
    return (xp.reshape(bp, sp, d), xs.reshape(bs, ts, d),
            kv_p[0].reshape(n_moba, bp, sp, m_heads, m_hd), kv_p[1].reshape(n_moba, bp, sp, m_heads, m_hd),
            kv_s[0].reshape(n_moba, bs, ts, m_heads, m_hd), kv_s[1].reshape(n_moba, bs, ts, m_heads, m_hd),
            gla_p, gla_s, jnp.stack(memk), jnp.stack(memv))
```

```python
import functools
import math

import numpy as np
import jax
import jax.numpy as jnp
from jax import lax
from jax.experimental import pallas as pl
from jax.experimental.pallas import tpu as pltpu

F32 = jnp.float32
BF16 = jnp.bfloat16

EPS = 1e-6
NEG_INF = -1e30
N_MIXERS = 2
GLA_HEADS = 4
GLA_GATE_TAU = 16.0
GLA_CHUNK = 64
MOBA_HEADS = 8
MOBA_BLOCK = 256
MOBA_TOPK = 3
X_HEADS = 4

LANES = 128
ONES_ROWS = 16
V7X_VMEM_BYTES = 64 << 20
VMEM_CAP_BYTES = V7X_VMEM_BYTES - (8 << 20)


def _cparams(sem, vmem_bytes):
    return pltpu.CompilerParams(dimension_semantics=sem,
                                vmem_limit_bytes=int(min(max(vmem_bytes, 16 << 20), VMEM_CAP_BYTES)))


def _row_tile(rows, pref):
    t = min(rows, pref)
    while rows % t:
        t //= 2
    return t


def _resident(shape, index_map):
    return pl.BlockSpec(shape, index_map, pipeline_mode=pl.Buffered(1))


def _layer_weight(w):
    stacked, layer = w
    _, k, n = stacked.shape
    return stacked, pl.BlockSpec((None, k, n), lambda *_: (layer, 0, 0), pipeline_mode=pl.Buffered(1)), (k, n)


def _rms(x, g):
    return x * lax.rsqrt(jnp.mean(x * x, axis=-1, keepdims=True) + EPS) * g


def _split3(a):
    a1 = a.astype(BF16).astype(F32)
    r = a - a1
    a2 = r.astype(BF16).astype(F32)
    a3 = (r - a2).astype(BF16).astype(F32)
    return a1, a2, a3


def _dot_nt(a, b):
    return lax.dot_general(a, b, (((1,), (1,)), ((), ())), preferred_element_type=F32)


def _dot_tn(a, b):
    return lax.dot_general(a, b, (((0,), (0,)), ((), ())), preferred_element_type=F32)


def _dot_f32(a, b, nt=False):
    a1, a2, a3 = _split3(a)
    b1, b2, b3 = _split3(b)
    mm = _dot_nt if nt else functools.partial(jnp.dot, preferred_element_type=F32)
    cast = lambda t: t.astype(BF16)
    small = mm(cast(a1), cast(b3)) + mm(cast(a2), cast(b2)) + mm(cast(a3), cast(b1))
    mid = mm(cast(a1), cast(b2)) + mm(cast(a2), cast(b1))
    return mm(cast(a1), cast(b1)) + (mid + small)


def _heads_to_rows(x, heads):
    hd = x.shape[1] // heads
    return jnp.concatenate([x[:, hh * hd:(hh + 1) * hd] for hh in range(heads)], axis=0)


def _rows_to_heads(x, heads):
    t = x.shape[0] // heads
    return jnp.concatenate([x[hh * t:(hh + 1) * t, :] for hh in range(heads)], axis=1)


def _norm_matmul_kernel(x_ref, g_ref, w_ref, *out_refs, splits):
    h = _rms(x_ref[...], g_ref[...]).astype(BF16)
    off = 0
    for o_ref, n in zip(out_refs, splits):
        o_ref[...] = jnp.dot(h, w_ref[:, off:off + n], preferred_element_type=F32)
        off += n


def norm_matmul(x, g, w, splits, row_tile=512):
    rows, d = x.shape
    w_bf16, w_spec, (_, n) = _layer_weight(w)
    assert sum(splits) == n
    tm = _row_tile(rows, row_tile)
    vmem = 2 * tm * d * 4 + d * n * 2 + 2 * tm * n * 4 + 2 * tm * max(splits) * 4 + (4 << 20)
    return pl.pallas_call(
        functools.partial(_norm_matmul_kernel, splits=tuple(splits)),
        grid=(rows // tm,),
        in_specs=[pl.BlockSpec((tm, d), lambda i: (i, 0)),
                  _resident((1, d), lambda i: (0, 0)),
                  w_spec],
        out_specs=[pl.BlockSpec((tm, s), lambda i: (i, 0)) for s in splits],
        out_shape=[jax.ShapeDtypeStruct((rows, s), F32) for s in splits],
        compiler_params=_cparams(("parallel",), vmem),
        name="norm_matmul",
    )(x, g.reshape(1, d), w_bf16)


def _matmul_residual_kernel(x_ref, a_ref, w_ref, o_ref):
    o_ref[...] = x_ref[...] + jnp.dot(a_ref[...].astype(BF16), w_ref[...], preferred_element_type=F32)


def matmul_residual(x, a, w, row_tile=512):
    rows, d = x.shape
    w_bf16, w_spec, (k, _) = _layer_weight(w)
    tm = _row_tile(rows, row_tile)
    vmem = 4 * tm * d * 4 + 2 * tm * k * 4 + k * d * 2 + tm * d * 4 + (4 << 20)
    return pl.pallas_call(
        _matmul_residual_kernel,
        grid=(rows // tm,),
        in_specs=[pl.BlockSpec((tm, d), lambda i: (i, 0)),
                  pl.BlockSpec((tm, k), lambda i: (i, 0)),
                  w_spec],
        out_specs=pl.BlockSpec((tm, d), lambda i: (i, 0)),
        out_shape=jax.ShapeDtypeStruct((rows, d), F32),
        compiler_params=_cparams(("parallel",), vmem),
        name="matmul_residual",
    )(x, a, w_bf16)


def _softmax_rows(s):
    e = jnp.exp(s - jnp.max(s, axis=-1, keepdims=True))
    return e / jnp.sum(e, axis=-1, keepdims=True)


def _xattn_rows(x, g, wq_ref, kv, wo_ref, heads):
    h = _rms(x, g).astype(BF16)
    q = jnp.dot(h, wq_ref[...], preferred_element_type=F32)
    xw = wq_ref.shape[1]
    hd = xw // heads
    head = lambda a, off, hh: a[:, off + hh * hd:off + (hh + 1) * hd].astype(BF16)
    scores = [_dot_nt(head(q, 0, hh), head(kv, 0, hh)) for hh in range(heads)]
    probs = [_softmax_rows(s * (hd ** -0.5)).astype(BF16) for s in scores]
    outs = [jnp.dot(probs[hh], head(kv, xw, hh), preferred_element_type=F32).astype(BF16) for hh in range(heads)]
    return jnp.dot(jnp.concatenate(outs, axis=1), wo_ref[...], preferred_element_type=F32)


def _mlp_rows(x, g, wu_ref, wd_ref, ff_tile):
    h = _rms(x, g).astype(BF16)
    acc = x
    for c in range(wu_ref.shape[1] // ff_tile):
        a = jnp.dot(h, wu_ref[:, c * ff_tile:(c + 1) * ff_tile], preferred_element_type=F32)
        a = jnp.maximum(a, 0.0)
        a = (a * a).astype(BF16)
        acc = acc + jnp.dot(a, wd_ref[c * ff_tile:(c + 1) * ff_tile, :], preferred_element_type=F32)
    return acc


def _post_mixer_kernel(*refs, heads, ff_tile, has_xattn, final_norm):
    refs = list(refs)
    x_ref, a_ref, wa_ref = refs[:3]
    del refs[:3]
    x = x_ref[...] + jnp.dot(a_ref[...].astype(BF16), wa_ref[...], preferred_element_type=F32)
    if has_xattn:
        gx_ref, wq_ref, kv_ref, wxo_ref = refs[:4]
        del refs[:4]
        x = x + _xattn_rows(x, gx_ref[...], wq_ref, kv_ref[...], wxo_ref, heads)
    gm_ref, wu_ref, wd_ref = refs[:3]
    del refs[:3]
    x = _mlp_rows(x, gm_ref[...], wu_ref, wd_ref, ff_tile)
    if final_norm:
        x = _rms(x, refs.pop(0)[...])
    (o_ref,) = refs
    o_ref[...] = x


def post_mixer(x, a, wa, xattn, g_mlp, wu, wd, g_final=None, row_tile=512, ff_tile=1024):
    rows, d = x.shape
    wa_arr, wa_spec, (ka, _) = _layer_weight(wa)
    wu_arr, wu_spec, (_, d_ff) = _layer_weight(wu)
    wd_arr, wd_spec, _ = _layer_weight(wd)
    has_xattn = xattn is not None
    final_norm = g_final is not None
    tm = _row_tile(xattn[4] if has_xattn else rows, row_tile)
    const = lambda i: (0, 0)
    in_specs = [pl.BlockSpec((tm, d), lambda i: (i, 0)),
                pl.BlockSpec((tm, ka), lambda i: (i, 0)),
                wa_spec]
    args = [x, a, wa_arr]
    vmem = 4 * tm * d * 4 + 2 * tm * ka * 4 + ka * d * 2 + 2 * d * d_ff * 2 + tm * ff_tile * 6 + 4 * tm * d * 4 + (4 << 20)
    if has_xattn:
        g_cross, wq, memkv, wxo, seq = xattn
        wq_arr, wq_spec, (_, xw) = _layer_weight(wq)
        wxo_arr, wxo_spec, _ = _layer_weight(wxo)
        mem = memkv.shape[1]
        per_seq = seq // tm
        in_specs += [_resident((1, d), const), wq_spec,
                     pl.BlockSpec((None, mem, 2 * xw), lambda i: (i // per_seq, 0, 0)),
                     wxo_spec]
        args += [g_cross.reshape(1, d), wq_arr, memkv, wxo_arr]
        vmem += 2 * d * xw * 2 + 2 * mem * 2 * xw * 4 + 6 * tm * xw * 4
    in_specs += [_resident((1, d), const), wu_spec, wd_spec]
    args += [g_mlp.reshape(1, d), wu_arr, wd_arr]
    if final_norm:
        in_specs.append(_resident((1, d), const))
        args.append(g_final.reshape(1, d))
    return pl.pallas_call(
        functools.partial(_post_mixer_kernel, heads=X_HEADS, ff_tile=ff_tile, has_xattn=has_xattn,
                          final_norm=final_norm),
        grid=(rows // tm,),
        in_specs=in_specs,
        out_specs=pl.BlockSpec((tm, d), lambda i: (i, 0)),
        out_shape=jax.ShapeDtypeStruct((rows, d), F32),
        compiler_params=_cparams(("parallel",), vmem),
        name="post_mixer",
    )(*args)


def _xattn_sample_kernel(q_ref, k_ref, v_ref, o_ref, *, heads):
    group, t, xw = q_ref.shape
    hd = xw // heads
    rows = heads * t
    n_kv = k_ref.shape[1]
    row_head = lax.broadcasted_iota(jnp.int32, (rows, 1), 0) // t
    own_head = (lax.broadcasted_iota(jnp.int32, (rows, n_kv), 1) % heads) == row_head
    scores = [_dot_nt(_heads_to_rows(q_ref[gi], heads).astype(BF16), k_ref[gi].astype(BF16))
              for gi in range(group)]
    probs = [_softmax_rows(jnp.where(own_head, s * (hd ** -0.5), NEG_INF)).astype(BF16) for s in scores]
    for gi in range(group):
        o = jnp.dot(probs[gi], v_ref[gi].astype(BF16), preferred_element_type=F32)
        o_ref[gi] = _rows_to_heads(o, heads)


def xattn_sample(q, mem_k, mem_v, layer, group=8):
    bs, t, xw = q.shape
    n_kv, hd = mem_k.shape[2], mem_k.shape[3]
    g = _row_tile(bs, group)
    vmem = 4 * g * n_kv * hd * 4 + 2 * n_kv * hd * 2 + 6 * X_HEADS * t * n_kv * 4 + 4 * g * t * xw * 4 + (4 << 20)
    kv_spec = pl.BlockSpec((None, g, n_kv, hd), lambda i: (layer, i, 0, 0))
    return pl.pallas_call(
        functools.partial(_xattn_sample_kernel, heads=X_HEADS),
        grid=(bs // g,),
        in_specs=[pl.BlockSpec((g, t, xw), lambda i: (i, 0, 0)), kv_spec, kv_spec],
        out_specs=pl.BlockSpec((g, t, xw), lambda i: (i, 0, 0)),
        out_shape=jax.ShapeDtypeStruct((bs, t, xw), F32),
        compiler_params=_cparams(("parallel",), vmem),
        name="xattn_sample",
    )(q, mem_k, mem_v)


def _log_sigmoid(z):
    return -(jnp.maximum(-z, 0.0) + jnp.log1p(jnp.exp(-jnp.abs(z))))


def _gla_inproj_kernel(x_ref, g_ref, w_ref, wg2_ref, bg_ref, y_ref, lf_ref, *, n_main):
    h = _rms(x_ref[...], g_ref[...]).astype(BF16)
    for c in range(0, n_main, 1024):
        y_ref[:, c:c + 1024] = jnp.dot(h, w_ref[:, c:c + 1024], preferred_element_type=F32)
    t = jnp.dot(h, w_ref[:, n_main:], preferred_element_type=F32).astype(BF16)
    z = jnp.dot(t, wg2_ref[...], preferred_element_type=F32) + bg_ref[...]
    lf_ref[...] = _log_sigmoid(z) / GLA_GATE_TAU


def gla_inproj(x, g, w_ext_bf16, wg2_bf16, b_g, row_tile=512):
    rows, d = x.shape
    n_ext = w_ext_bf16.shape[1]
    n_main = n_ext - LANES
    dk_tot = wg2_bf16.shape[1]
    tm = _row_tile(rows, row_tile)
    vmem = 2 * tm * d * 4 + d * n_ext * 2 + 2 * tm * (n_main + dk_tot) * 4 + 2 * tm * 1024 * 4 + (4 << 20)
    return pl.pallas_call(
        functools.partial(_gla_inproj_kernel, n_main=n_main),
        grid=(rows // tm,),
        in_specs=[pl.BlockSpec((tm, d), lambda i: (i, 0)),
                  _resident((1, d), lambda i: (0, 0)),
                  _resident((d, n_ext), lambda i: (0, 0)),
                  _resident((LANES, dk_tot), lambda i: (0, 0)),
                  _resident((1, dk_tot), lambda i: (0, 0))],
        out_specs=[pl.BlockSpec((tm, n_main), lambda i: (i, 0)),
                   pl.BlockSpec((tm, dk_tot), lambda i: (i, 0))],
        out_shape=[jax.ShapeDtypeStruct((rows, n_main), F32),
                   jax.ShapeDtypeStruct((rows, dk_tot), F32)],
        compiler_params=_cparams(("parallel",), vmem),
        name="gla_inproj",
    )(x, g.reshape(1, d), w_ext_bf16, wg2_bf16, b_g.reshape(1, dk_tot))


def _gla_kernel(q_ref, k_ref, v_ref, r_ref, lf_ref, gout_ref, *rest, chunk, n_sub, heads, has_s0, aliased):
    rest = list(rest)
    s0_ref = rest.pop(0) if has_s0 else None
    if aliased:
        rest.pop(0)
    o_ref, s_ref = rest
    group = q_ref.shape[0]
    dk = q_ref.shape[2] // heads
    dv = v_ref.shape[2] // heads

    @pl.when(pl.program_id(1) == 0)
    def _():
        if has_s0:
            s_ref[...] = s0_ref[...]
        else:
            s_ref[...] = jnp.zeros(s_ref.shape, F32)

    op = (lambda a: a.astype(BF16)) if chunk >= 16 else (lambda a: a.astype(BF16).astype(F32))
    row = lax.broadcasted_iota(jnp.int32, (chunk, chunk), 0)
    col = lax.broadcasted_iota(jnp.int32, (chunk, chunk), 1)
    tril = row >= col
    ltri = op(jnp.where(tril, 1.0, 0.0).astype(F32))
    eye = (lax.broadcasted_iota(jnp.int32, (dk, dk), 0) == lax.broadcasted_iota(jnp.int32, (dk, dk), 1))
    mm = functools.partial(jnp.dot, preferred_element_type=F32)

    units = [(bi, hh) for bi in range(group) for hh in range(heads)]
    for u in range(n_sub):
        rs = slice(u * chunk, (u + 1) * chunk)
        gcum = []
        for bi in range(group):
            g1, g2, g3 = _split3(lf_ref[bi, rs, :])
            gcum.append(mm(ltri, op(g1)) + (mm(ltri, op(g2)) + mm(ltri, op(g3))))
        q_dec, k_dec, k_rem, vb, g_last = {}, {}, {}, {}, {}
        for bi, hh in units:
            ks = slice(hh * dk, (hh + 1) * dk)
            gh = gcum[bi][:, ks]
            kh = k_ref[bi, rs, ks]
            g_last[bi, hh] = gh[chunk - 1:chunk, :]
            q_dec[bi, hh] = op(q_ref[bi, rs, ks] * (dk ** -0.5) * jnp.exp(gh))
            k_dec[bi, hh] = op(kh * jnp.exp(-gh))
            k_rem[bi, hh] = op(kh * jnp.exp(g_last[bi, hh] - gh))
            vb[bi, hh] = op(v_ref[bi, rs, hh * dv:(hh + 1) * dv])
        a_raw = {un: _dot_nt(q_dec[un], k_dec[un]) for un in units}
        s_add = {un: _dot_tn(k_rem[un], vb[un]) for un in units}
        o_inter = {un: mm(q_dec[un], op(s_ref[un[0], un[1]])) for un in units}
        o_intra = {un: mm(op(jnp.where(tril, a_raw[un], 0.0)), vb[un]) for un in units}
        for bi, hh in units:
            vs = slice(hh * dv, (hh + 1) * dv)
            decay = jnp.sum(jnp.where(eye, jnp.broadcast_to(jnp.exp(g_last[bi, hh]), (dk, dk)), 0.0),
                            axis=1, keepdims=True)
            s_ref[bi, hh] = decay * s_ref[bi, hh] + s_add[bi, hh]
            on = _rms(o_inter[bi, hh] + o_intra[bi, hh], gout_ref[:, vs])
            rh = r_ref[bi, rs, vs]
            o_ref[bi, rs, vs] = on * (rh * jax.nn.sigmoid(rh))


def gla_core(y, logf, g_out, s0_all, layer, states, n_layers, batch, seq, row_tile=128, group=4):
    rows, n = y.shape
    dk_tot = logf.shape[1]
    dv_tot = (n - 2 * dk_tot) // 2
    assert dv_tot == 2 * dk_tot
    heads = GLA_HEADS
    dk, dv = dk_tot // heads, dv_tot // heads
    chunk = math.gcd(seq, GLA_CHUNK)
    tl = _row_tile(seq, max(row_tile, chunk))
    n_sub = tl // chunk
    per_seq = seq // tl
    grp = _row_tile(batch, group)
    has_s0 = s0_all is not None
    aliased = states is not None
    y = y.reshape(batch, seq, n)
    logf = logf.reshape(batch, seq, dk_tot)
    state_spec = pl.BlockSpec((None, grp, heads, dk, dv), lambda b, c: (layer, b, 0, 0, 0))
    in_specs = [pl.BlockSpec((grp, tl, dk_tot), lambda b, c: (b, c, 0)),
                pl.BlockSpec((grp, tl, dk_tot), lambda b, c: (b, c, 1)),
                pl.BlockSpec((grp, tl, dv_tot), lambda b, c: (b, c, 1)),
                pl.BlockSpec((grp, tl, dv_tot), lambda b, c: (b, c, 2)),
                pl.BlockSpec((grp, tl, dk_tot), lambda b, c: (b, c, 0)),
                _resident((1, dv_tot), lambda b, c: (0, 0))]
    args = [y, y, y, y, logf, g_out.reshape(1, dv_tot)]
    if has_s0:
        in_specs.append(state_spec)
        args.append(s0_all)
    aliases = {}
    if aliased:
        aliases = {len(args): 1}
        in_specs.append(pl.BlockSpec(memory_space=pl.ANY))
        args.append(states)
    state_bytes = grp * heads * dk * dv * 4
    vmem = grp * (2 * tl * (3 * dk_tot + 2 * dv_tot) * 4 + 2 * tl * dv_tot * 4) + 4 * state_bytes + (16 << 20)
    og, new_states = pl.pallas_call(
        functools.partial(_gla_kernel, chunk=chunk, n_sub=n_sub, heads=heads, has_s0=has_s0, aliased=aliased),
        grid=(batch // grp, per_seq),
        in_specs=in_specs,
        out_specs=[pl.BlockSpec((grp, tl, dv_tot), lambda b, c: (b, c, 0)), state_spec],
        out_shape=[jax.ShapeDtypeStruct((batch, seq, dv_tot), F32),
                   jax.ShapeDtypeStruct((n_layers, batch, heads, dk, dv), F32)],
        input_output_aliases=aliases,
        compiler_params=_cparams(("parallel", "arbitrary"), vmem),
        name="gla_core",
    )(*args)
    return og.reshape(rows, dv_tot), new_states


def _alibi_slopes(n):
    return np.asarray(2.0 ** (-8.0 * np.arange(1, n + 1) / n), dtype=np.float32)


def _moba_qkv_kernel(x_ref, g_ref, w_ref, *rest, heads, dense_kv, aliased):
    rest = list(rest)
    if aliased:
        del rest[:2]
    q_ref = rest.pop(0)
    if dense_kv:
        k_ref, v_ref = rest.pop(0), rest.pop(0)
    kn_ref, vn_ref = rest
    tm = x_ref.shape[0]
    width = q_ref.shape[1]
    hd = width // heads
    h = _rms(x_ref[...], g_ref[...]).astype(BF16)
    q_ref[...] = jnp.dot(h, w_ref[:, :width], preferred_element_type=F32)
    k = jnp.dot(h, w_ref[:, width:2 * width], preferred_element_type=F32)
    v = jnp.dot(h, w_ref[:, 2 * width:], preferred_element_type=F32)
    if dense_kv:
        k_ref[...] = k
        v_ref[...] = v
    for hh in range(heads):
        kn_ref[pl.ds(hh, tm, stride=heads), :] = k[:, hh * hd:(hh + 1) * hd]
        vn_ref[pl.ds(hh, tm, stride=heads), :] = v[:, hh * hd:(hh + 1) * hd]


def moba_qkv(x, g, w, kv_rows, dense_kv, row_tile=512):
    rows, d = x.shape
    w_bf16, w_spec, (_, n) = _layer_weight(w)
    n_layers, layer = w[0].shape[0], w[1]
    width = n // 3
    heads = MOBA_HEADS
    hd = width // heads
    tm = _row_tile(rows, row_tile)
    aliased = kv_rows is not None
    n_dense = 3 if dense_kv else 1
    in_specs = [pl.BlockSpec((tm, d), lambda i: (i, 0)),
                _resident((1, d), lambda i: (0, 0)),
                w_spec]
    args = [x, g.reshape(1, d), w_bf16]
    aliases = {}
    if aliased:
        aliases = {3: n_dense, 4: n_dense + 1}
        in_specs += [pl.BlockSpec(memory_space=pl.ANY)] * 2
        args += list(kv_rows)
    dense_spec = pl.BlockSpec((tm, width), lambda i: (i, 0))
    rows_spec = pl.BlockSpec((None, tm * heads, hd), lambda i: (layer, i, 0))
    vmem = 2 * tm * d * 4 + d * 3 * width * 2 + (2 * n_dense + 4 + 3) * tm * width * 4 + (4 << 20)
    return pl.pallas_call(
        functools.partial(_moba_qkv_kernel, heads=heads, dense_kv=dense_kv, aliased=aliased),
        grid=(rows // tm,),
        in_specs=in_specs,
        out_specs=[dense_spec] * n_dense + [rows_spec] * 2,
        out_shape=[jax.ShapeDtypeStruct((rows, width), F32)] * n_dense
                  + [jax.ShapeDtypeStruct((n_layers, rows * heads, hd), F32)] * 2,
        input_output_aliases=aliases,
        compiler_params=_cparams(("parallel",), vmem),
        name="moba_qkv",
    )(*args)


def _topk_mask(gate, block_ids, valid, topk, n_blocks, axis):
    gm = jnp.where(valid, gate, NEG_INF)
    rank = jnp.zeros(gate.shape, F32)
    for m in range(n_blocks):
        gsel = gm[m:m + 1, :] if axis == 0 else gm[:, m:m + 1]
        ahead = jnp.where(gsel > gm, 1.0, jnp.where(gsel == gm, jnp.where(block_ids > m, 1.0, 0.0), 0.0))
        rank = rank + ahead
    return jnp.where(valid, jnp.where(rank < topk, 1.0, 0.0), 0.0)


def _topk_mask_lanes(gate, valid, topk, n_blocks, stride):
    gm = jnp.where(valid, gate, NEG_INF)
    rank = jnp.zeros(gate.shape, F32)
    for r in range(1, n_blocks):
        lower = pltpu.roll(gm, r * stride, axis=1)
        upper = pltpu.roll(gm, LANES - r * stride, axis=1)
        rank = rank + jnp.where(lower >= gm, 1.0, 0.0) + jnp.where(upper > gm, 1.0, 0.0)
    return jnp.where(valid, jnp.where(rank < topk, 1.0, 0.0), 0.0)


def _moba_prompt_kernel(slopes_ref, zero_ref, q_ref, k_ref, v_ref, o_ref, kb_ref, vt_ref, s0_ref, s1_ref, bias_ref,
                        *, n_blocks, topk):
    blk = MOBA_BLOCK
    hd = q_ref.shape[1]
    scale = hd ** -0.5
    slope = slopes_ref[pl.program_id(1)]
    key_idx = lax.broadcasted_iota(jnp.int32, (blk, blk), 0)
    qry_idx = lax.broadcasted_iota(jnp.int32, (blk, blk), 1)
    causal = key_idx <= qry_idx
    bias_ref[...] = -slope * (qry_idx - key_idx).astype(F32)
    mm = functools.partial(jnp.dot, preferred_element_type=F32)

    nbp = -(-n_blocks // 8) * 8
    k_means = []
    for n in range(n_blocks):
        kblk = k_ref[n * blk:(n + 1) * blk, :]
        kb_ref[n] = kblk.astype(BF16)
        k_means.append(jnp.mean(kblk, axis=0, keepdims=True))
        vt_ref[n, :hd, :] = v_ref[n * blk:(n + 1) * blk, :].T.astype(BF16)
        vt_ref[n, hd:, :] = jnp.ones((ONES_ROWS, blk), BF16)
    if nbp > n_blocks:
        k_means.append(jnp.zeros((nbp - n_blocks, hd), F32))
    km = jnp.concatenate(k_means, axis=0)
    block_ids = lax.broadcasted_iota(jnp.int32, (nbp, blk), 0)

    def begin(i):
        q_t = q_ref[i * blk:(i + 1) * blk, :].T
        st = dict(i=i, buf=(s0_ref, s1_ref)[i % 2], chosen=[], dist=[], top=None, acc=None,
                  q_tb=(q_t * scale).astype(BF16))
        if i > 0:
            gate = _dot_f32(km, q_t)
            st["sel"] = _topk_mask(gate, block_ids, block_ids < i, topk, i, axis=0)
        return st

    def logits(st, j):
        i = st["i"]
        s = mm(kb_ref[j], st["q_tb"]) + bias_ref[...]
        if j == i:
            s = jnp.where(causal, s, NEG_INF)
        st["buf"][j] = s
        mb = jnp.max(s, axis=0, keepdims=True)
        if j < i:
            st["chosen"].append(st["sel"][j:j + 1, :] > 0.5)
            st["dist"].append(slope * float((i - j) * blk))
            mb = jnp.where(st["chosen"][j], mb - st["dist"][j], NEG_INF)
        st["top"] = mb if st["top"] is None else jnp.maximum(st["top"], mb)

    def weigh(st, j):
        top = st["top"]
        shift = jnp.where(st["chosen"][j], top + st["dist"][j], -NEG_INF) if j < st["i"] else top
        p = jnp.exp(st["buf"][j + zero_ref[0]] - shift)
        pv = mm(vt_ref[j], p.astype(BF16))
        st["acc"] = pv if st["acc"] is None else st["acc"] + pv

    cur = begin(0)
    logits(cur, 0)
    for i in range(n_blocks):
        nxt = begin(i + 1) if i + 1 < n_blocks else None
        for j in range(i + 2):
            if nxt is not None:
                logits(nxt, j)
            if j <= i:
                weigh(cur, j)
        acc = cur["acc"]
        o_ref[i * blk:(i + 1) * blk, :] = (acc[:hd, :] / acc[hd:hd + 1, :]).T
        cur = nxt


def moba_prompt(q, k, v, batch, seq):
    rows, width = q.shape
    heads = MOBA_HEADS
    hd = width // heads
    blk = MOBA_BLOCK
    assert seq % blk == 0 and hd == LANES
    n_blocks = seq // blk
    topk = min(MOBA_TOPK, n_blocks - 1)
    slopes = jnp.asarray(_alibi_slopes(heads))
    strip = pl.BlockSpec((seq, hd), lambda b, h, sl, zero: (b, h))
    grid_spec = pltpu.PrefetchScalarGridSpec(
        num_scalar_prefetch=2,
        grid=(batch, heads),
        in_specs=[strip, strip, strip],
        out_specs=strip,
        scratch_shapes=[pltpu.VMEM((n_blocks, blk, hd), BF16),
                        pltpu.VMEM((n_blocks, hd + ONES_ROWS, blk), BF16),
                        pltpu.VMEM((n_blocks, blk, blk), F32),
                        pltpu.VMEM((n_blocks, blk, blk), F32),
                        pltpu.VMEM((blk, blk), F32)])
    vmem = 8 * seq * hd * 4 + 2 * seq * hd * 2 + (2 * n_blocks + 12) * blk * blk * 4 + (8 << 20)
    return pl.pallas_call(
        functools.partial(_moba_prompt_kernel, n_blocks=n_blocks, topk=topk),
        grid_spec=grid_spec,
        out_shape=jax.ShapeDtypeStruct((rows, width), F32),
        compiler_params=_cparams(("parallel", "parallel"), vmem),
        name="moba_prompt",
    )(slopes, jnp.zeros((1,), jnp.int32), q, k, v)


RING_SLOTS = 3
RING_LEAD = RING_SLOTS - 1


def _moba_sample_kernel(pt_ref, q_ref, kn_ref, vn_ref, slope_ref, *rest, n_blocks, ppb, topk, heads, past_len,
                        layer):
    n_pages = n_blocks * ppb
    ck_hbm, cv_hbm, o_ref, kbuf, vbuf, ksem, vsem = rest
    b = pl.program_id(0)
    n_seq = pl.num_programs(0)

    def page_copies(seq, slot):
        out = []
        for pg in range(n_pages):
            phys = pt_ref[seq, pg]
            out.append(pltpu.make_async_copy(ck_hbm.at[layer, phys], kbuf.at[slot, pg], ksem.at[slot]))
            out.append(pltpu.make_async_copy(cv_hbm.at[layer, phys], vbuf.at[slot, pg], vsem.at[slot]))
        return out

    @pl.when(b == 0)
    def _():
        for ahead in range(RING_LEAD):
            @pl.when(ahead < n_seq)
            def _():
                for cp in page_copies(ahead, ahead % RING_SLOTS):
                    cp.start()

    @pl.when(b + RING_LEAD < n_seq)
    def _():
        for cp in page_copies(b + RING_LEAD, (b + RING_LEAD) % RING_SLOTS):
            cp.start()

    slot = b % RING_SLOTS
    for cp in page_copies(b, slot):
        cp.wait()
    k_refs = [kbuf.at[slot, pg] for pg in range(n_pages)]
    v_refs = [vbuf.at[slot, pg] for pg in range(n_pages)]
    t_new, width = q_ref.shape
    hd = width // heads
    rows = heads * t_new
    page = kbuf.shape[2] // heads
    scale = hd ** -0.5
    row_iota = lax.broadcasted_iota(jnp.int32, (rows, 1), 0)
    row_head = row_iota // t_new
    row_t = row_iota % t_new
    mm = functools.partial(jnp.dot, preferred_element_type=F32)
    head_rows = lambda a, hh: a[hh * t_new:(hh + 1) * t_new, :]
    head_keys = lambda ref, hh: ref[pl.ds(hh, page, stride=heads), :]

    qa = _heads_to_rows(q_ref[...], heads)
    qab = qa.astype(BF16)
    q_heads = [head_rows(qa, hh).astype(BF16) for hh in range(heads)]
    slope = slope_ref[:, 0:1]
    q_pos = (past_len + row_t).astype(F32)
    lane_key = lax.broadcasted_iota(jnp.int32, (rows, page), 1)

    scores, k_means = [], []
    for b in range(n_blocks):
        km = jnp.zeros((heads, hd), F32)
        for pg in range(b * ppb, (b + 1) * ppb):
            k_heads = [head_keys(k_refs[pg], hh) for hh in range(heads)]
            km = km + jnp.concatenate([kh.mean(axis=0, keepdims=True) for kh in k_heads], axis=0)
            scores.append(jnp.concatenate(
                [_dot_nt(q_heads[hh], k_heads[hh].astype(BF16)) for hh in range(heads)], axis=0))
        k_means.append(km / ppb)
    k_means.append(jnp.zeros((LANES - n_blocks * heads, hd), F32))
    lane_g = lax.broadcasted_iota(jnp.int32, (rows, LANES), 1)
    gate = _dot_f32(qa, jnp.concatenate(k_means, axis=0), nt=True)
    valid = ((lane_g % heads) == row_head) & (lane_g < n_blocks * heads)
    sel = _topk_mask_lanes(gate, valid, topk, n_blocks, heads)
    chosen = [jnp.sum(sel[:, b * heads:(b + 1) * heads], axis=1, keepdims=True) > 0.5
              for b in range(n_blocks)]

    pad = jnp.zeros((LANES - t_new * heads, hd), F32)
    k_own = jnp.concatenate([kn_ref[...], pad], axis=0).astype(BF16)
    v_own = jnp.concatenate([vn_ref[...], pad], axis=0).astype(BF16)
    d_own = (row_t - lane_g // heads).astype(F32)
    s_own = _dot_nt(qab, k_own) * scale - slope * d_own
    s_own = jnp.where(((lane_g % heads) == row_head) & (d_own >= 0), s_own, NEG_INF)
    top = jnp.max(s_own, axis=1, keepdims=True)

    logits = []
    for pg in range(n_pages):
        key_pos = (pg * page + lane_key).astype(F32)
        lg = jnp.where(chosen[pg // ppb], scores[pg] * scale - slope * (q_pos - key_pos), NEG_INF)
        top = jnp.maximum(top, jnp.max(lg, axis=1, keepdims=True))
        logits.append(lg)
    p_own = jnp.exp(s_own - top)
    l = jnp.sum(p_own, axis=1, keepdims=True)
    probs = []
    for pg in range(n_pages):
        p = jnp.exp(logits[pg] - top)
        l = l + jnp.sum(p, axis=1, keepdims=True)
        probs.append(p)
    outs = []
    for hh in range(heads):
        acc = None
        for pg in range(n_pages):
            pv = mm(head_rows(probs[pg], hh).astype(BF16), head_keys(v_refs[pg], hh).astype(BF16))
            acc = pv if acc is None else acc + pv
        outs.append(acc)
    acc = jnp.concatenate(outs, axis=0) + mm(p_own.astype(BF16), v_own)
    o_ref[...] = _rows_to_heads(acc / l, heads)


def moba_sample(q, kv_rows, cache_k, cache_v, layer, page_table):
    bs, t_new, width = q.shape
    heads = MOBA_HEADS
    hd = width // heads
    page = cache_k.shape[2]
    n_pages = page_table.shape[1]
    past_len = n_pages * page
    ppb = MOBA_BLOCK // page
    n_blocks = past_len // MOBA_BLOCK
    assert n_blocks * ppb == n_pages and n_blocks >= 1
    assert t_new * heads <= LANES and 2 * n_blocks * heads <= LANES and heads % 8 == 0
    topk = min(MOBA_TOPK, n_blocks)
    rows = heads * t_new
    slope_rows = jnp.asarray(np.repeat(_alibi_slopes(heads), t_new)[:, None] * np.ones((1, LANES), np.float32))
    cache_k = cache_k.reshape(cache_k.shape[0], cache_k.shape[1], page * heads, hd)
    cache_v = cache_v.reshape(cache_v.shape[0], cache_v.shape[1], page * heads, hd)

    new_spec = pl.BlockSpec((None, rows, hd), lambda b, pt: (layer, b, 0))
    seq_spec = pl.BlockSpec((None, t_new, width), lambda b, pt: (b, 0, 0))
    hbm = pl.BlockSpec(memory_space=pl.ANY)
    ring = pltpu.VMEM((RING_SLOTS, n_pages, page * heads, hd), F32)
    grid_spec = pltpu.PrefetchScalarGridSpec(
        num_scalar_prefetch=1,
        grid=(bs,),
        in_specs=[seq_spec, new_spec, new_spec, _resident((rows, LANES), lambda b, pt: (0, 0)), hbm, hbm],
        out_specs=seq_spec,
        scratch_shapes=[ring, ring, pltpu.SemaphoreType.DMA((RING_SLOTS,)), pltpu.SemaphoreType.DMA((RING_SLOTS,))])
    vmem = 2 * RING_SLOTS * n_pages * page * width * 4 + 6 * n_pages * rows * page * 4 + (4 << 20)
    return pl.pallas_call(
        functools.partial(_moba_sample_kernel, n_blocks=n_blocks, ppb=ppb, topk=topk, heads=heads,
                          past_len=past_len, layer=layer),
        grid_spec=grid_spec,
        out_shape=jax.ShapeDtypeStruct((bs, t_new, width), F32),
        compiler_params=_cparams(("arbitrary",), vmem),
        name="moba_sample",
    )(page_table, q, kv_rows[0], kv_rows[1], slope_rows, cache_k, cache_v)


def kernel(x_prompt, x_sample, cache_moba_k, cache_moba_v, state_gla, cache_mem_k, cache_mem_v, page_table, mem_prompt, g_mix, g_cross, g_mlp, g_final, w_gla_in, w_gla_g1, w_gla_g2, b_gla_g, g_gla_out, w_gla_o, w_moba_qkv, w_moba_o, g_mem, w_mem_kv, w_xq, w_xo, w_up, w_down):
    bp, sp, d = x_prompt.shape
    bs, ts, _ = x_sample.shape
    depth = g_mix.shape[0]
    xp = x_prompt.reshape(bp * sp, d)
    xs = x_sample.reshape(bs * ts, d)
    bf = lambda w: w.astype(BF16)

    n_gla = state_gla.shape[0]
    n_moba, _, _, m_heads, m_hd = cache_moba_k.shape
    m_width = m_heads * m_hd
    mem_len = mem_prompt.shape[1]
    mem_rows = mem_prompt.reshape(bp * mem_len, d)
    x_hd = cache_mem_k.shape[4]
    xw = w_xq.shape[2]
    mem_k_rows = cache_mem_k.reshape(depth, bs, mem_len * X_HEADS, x_hd)
    mem_v_rows = cache_mem_v.reshape(depth, bs, mem_len * X_HEADS, x_hd)
    rank = w_gla_g1.shape[2]
    wo_gla, wqkv, wo_moba, w_memkv = bf(w_gla_o), bf(w_moba_qkv), bf(w_moba_o), bf(w_mem_kv)
    wq, wxo, wu, wd = bf(w_xq), bf(w_xo), bf(w_up), bf(w_down)

    gla_p = gla_s = kv_p = kv_s = None
    memk, memv = [], []
    for i in range(depth):
        j = i // N_MIXERS
        if i % N_MIXERS == 0:
            w_ext = bf(jnp.concatenate([w_gla_in[j], jnp.pad(w_gla_g1[j], ((0, 0), (0, LANES - rank)))], axis=1))
            wg2 = bf(jnp.pad(w_gla_g2[j], ((0, LANES - rank), (0, 0))))
            wo = (wo_gla, j)
            yp, lfp = gla_inproj(xp, g_mix[i], w_ext, wg2, b_gla_g[j])
            mix_p, gla_p = gla_core(yp, lfp, g_gla_out[j], None, j, gla_p, n_gla, bp, sp)
            ys, lfs = gla_inproj(xs, g_mix[i], w_ext, wg2, b_gla_g[j])
            ogs, gla_s = gla_core(ys, lfs, g_gla_out[j], state_gla, j, gla_s, n_gla, bs, ts)
            xs = matmul_residual(xs, ogs, wo)
        else:
            wo = (wo_moba, j)
            qp, kp, vp, *kv_p = moba_qkv(xp, g_mix[i], (wqkv, j), kv_p, dense_kv=True)
            mix_p = moba_prompt(qp, kp, vp, bp, sp)
            qs, *kv_s = moba_qkv(xs, g_mix[i], (wqkv, j), kv_s, dense_kv=False)
            a_s = moba_sample(qs.reshape(bs, ts, m_width), kv_s, cache_moba_k, cache_moba_v, j, page_table)
            xs = matmul_residual(xs, a_s.reshape(bs * ts, m_width), wo)
        (memkv,) = norm_matmul(mem_rows, g_mem[i], (w_memkv, i), (2 * xw,))
        memkv = memkv.reshape(bp, mem_len, 2 * xw)
        memk.append(memkv[:, :, :xw].reshape(bp, mem_len, X_HEADS, x_hd))
        memv.append(memkv[:, :, xw:].reshape(bp, mem_len, X_HEADS, x_hd))
        gf = g_final if i == depth - 1 else None
        xp = post_mixer(xp, mix_p, wo, (g_cross[i], (wq, i), memkv, (wxo, i), sp), g_mlp[i], (wu, i), (wd, i), gf)
        (qx,) = norm_matmul(xs, g_cross[i], (wq, i), (xw,))
        ax = xattn_sample(qx.reshape(bs, ts, xw), mem_k_rows, mem_v_rows, i)
        xs = post_mixer(xs, ax.reshape(bs * ts, xw), (wxo, i), None, g_mlp[i], (wu, i), (wd, i), gf)
    return (xp.reshape(bp, sp, d), xs.reshape(bs, ts, d),
            kv_p[0].reshape(n_moba, bp, sp, m_heads, m_hd), kv_p[1].reshape(n_moba, bp, sp, m_heads, m_hd),
            kv_s[0].reshape(n_moba, bs, ts, m_heads, m_hd), kv_s[1].reshape(n_moba, bs, ts, m_heads, m_hd),
            gla_p, gla_s, jnp.stack(memk), jnp.stack(memv))
```
